```python
import jax, jax.numpy as jnp
from jax import lax
import numpy as np

D_MODEL = 1024
BATCH = 4
SEQ = 8192
DEPTH = 4

CTX_LEN = 256
GRID_W = 64
HEAD_DIM = 64
N_HEADS = D_MODEL // HEAD_DIM
NA_HEADS = N_HEADS // 2
WA_HEADS = N_HEADS - NA_HEADS
WA_KV_HEADS = max(1, WA_HEADS // 4)
NA_KH = 8
NA_KW = 16
WA_WINDOW = 128
WA_BLOCK = 128
D_FF = ((8 * D_MODEL // 3 + 127) // 128) * 128
ROPE_BASE = 10000.0
N_MOD = 9
MACARON_W = 0.5
NORM_EPS = 1e-6
NEG_INF = -1e30
NA_W = NA_HEADS * HEAD_DIM
WA_W = WA_HEADS * HEAD_DIM
WA_KV_W = WA_KV_HEADS * HEAD_DIM
Q_W = NA_W + WA_W
IN_W = Q_W + 2 * NA_W + 2 * WA_KV_W

kernel_name = "hybrid_natten_swa_macaron_dit_block"


def rms_norm(x, g):
    xf = x.astype(jnp.float32)
    y = xf * lax.rsqrt(jnp.mean(xf * xf, axis=-1, keepdims=True) + NORM_EPS)
    return (y * g.astype(jnp.float32)).astype(x.dtype)


def modulate(h, shift, scale):
    return h * (1 + scale) + shift


def ada_params(cond, w_ada, b_ada):
    m = jax.nn.silu(cond) @ w_ada + b_ada
    return m.reshape(m.shape[:-1] + (N_MOD, D_MODEL))


def swiglu_half_step(x, g, mod, j, w_up, w_down):
    h = modulate(rms_norm(x, g), mod[..., j, :], mod[..., j + 1, :])
    gt, up = jnp.split(h @ w_up, 2, axis=-1)
    return x + MACARON_W * mod[..., j + 2, :] * ((jax.nn.silu(gt) * up) @ w_down)


def heads(t):
    return t.reshape(t.shape[:-1] + (t.shape[-1] // HEAD_DIM, HEAD_DIM))


def axial_rope_tables(n_tokens):
    rot = HEAD_DIM // 2
    inv_freq = ROPE_BASE ** (-jnp.arange(0, rot, 2, dtype=jnp.float32) / rot)
    t = jnp.arange(n_tokens)
    row = (t // GRID_W).astype(jnp.float32)
    col = (t % GRID_W).astype(jnp.float32)
    ang = jnp.stack([row[:, None] * inv_freq, col[:, None] * inv_freq], axis=1)
    return jnp.cos(ang), jnp.sin(ang)


def apply_axial_rope(x, cos, sin):
    B, S, H, Dh = x.shape
    xr = x.reshape(B, S, H, 2, 2, Dh // 4)
    x1, x2 = xr[..., 0, :], xr[..., 1, :]
    c = cos[:, None].astype(x.dtype)
    s = sin[:, None].astype(x.dtype)
    out = jnp.stack([x1 * c - x2 * s, x2 * c + x1 * s], axis=-2)
    return out.reshape(B, S, H, Dh)


def context_self_attention(q, k, v, sink):
    B, C, Hq, Dh = q.shape
    Hkv = k.shape[2]
    G = Hq // Hkv
    qg = (q * Dh ** -0.5).reshape(B, C, Hkv, G, Dh)
    s = jnp.einsum('bqkgd,bckd->bkgqc', qg, k).astype(jnp.float32)
    if sink is not None:
        sink_col = jnp.broadcast_to(sink.astype(jnp.float32).reshape(1, Hkv, G, 1, 1), s.shape[:-1] + (1,))
        s = jnp.concatenate([s, sink_col], axis=-1)
    p = jax.nn.softmax(s, axis=-1)[..., :C].astype(v.dtype)
    out = jnp.einsum('bkgqc,bckd->bqkgd', p, v)
    return out.reshape(B, C, Hq * Dh)


def neighbourhood_attention(q, k, v, k_ctx, v_ctx, rpb):
    B, S, H, Dh = q.shape
    rows = S // GRID_W
    kh = min(NA_KH, rows)
    n_nb = kh * NA_KW
    qg = (q * Dh ** -0.5).reshape(B, rows, GRID_W, H, Dh)
    kg = k.reshape(B, rows, GRID_W, H, Dh)
    vg = v.reshape(B, rows, GRID_W, H, Dh)
    col = jnp.arange(GRID_W)
    c0 = jnp.clip(col - NA_KW // 2, 0, GRID_W - NA_KW)
    col_idx = c0[:, None] + jnp.arange(NA_KW)[None, :]
    dc = col_idx - col[:, None] + (NA_KW - 1)
    rpb32 = rpb.astype(jnp.float32)

    def one_row(r):
        r0 = jnp.clip(r - kh // 2, 0, rows - kh)
        q_r = lax.dynamic_index_in_dim(qg, r, axis=1, keepdims=False)
        k_rows = lax.dynamic_slice_in_dim(kg, r0, kh, axis=1)
        v_rows = lax.dynamic_slice_in_dim(vg, r0, kh, axis=1)
        k_win = k_rows[:, :, col_idx]
        v_win = v_rows[:, :, col_idx]
        dr = r0 + jnp.arange(kh) - r + (NA_KH - 1)
        bias = rpb32[:, dr[:, None, None], dc[None, :, :]].transpose(0, 2, 1, 3)
        s_nb = jnp.einsum('bqhd,biqjhd->bhqij', q_r, k_win).astype(jnp.float32) + bias[None]
        s_ctx = jnp.einsum('bqhd,bchd->bhqc', q_r, k_ctx).astype(jnp.float32)
        logits = jnp.concatenate([s_nb.reshape(B, H, GRID_W, n_nb), s_ctx], axis=-1)
        p = jax.nn.softmax(logits, axis=-1).astype(v.dtype)
        p_nb = p[..., :n_nb].reshape(B, H, GRID_W, kh, NA_KW)
        return (jnp.einsum('bhqij,biqjhd->bqhd', p_nb, v_win)
                + jnp.einsum('bhqc,bchd->bqhd', p[..., n_nb:], v_ctx))

    out = lax.map(one_row, jnp.arange(rows))
    return out.transpose(1, 0, 2, 3, 4).reshape(B, S, H * Dh)


def windowed_gqa_attention(q, k, v, k_ctx, v_ctx, sink):
    B, S, Hq, Dh = q.shape
    Hkv = k.shape[2]
    G = Hq // Hkv
    nb = S // WA_BLOCK
    n_loc = 3 * WA_BLOCK
    qb = (q * Dh ** -0.5).reshape(B, nb, WA_BLOCK, Hkv, G, Dh)
    pad = ((0, 0), (WA_BLOCK, WA_BLOCK), (0, 0), (0, 0))
    kp = jnp.pad(k, pad)
    vp = jnp.pad(v, pad)
    q_off = jnp.arange(WA_BLOCK)
    k_off = jnp.arange(n_loc) - WA_BLOCK
    band = jnp.abs(k_off[None, :] - q_off[:, None]) <= WA_WINDOW
    sink_col = jnp.broadcast_to(sink.astype(jnp.float32).reshape(1, Hkv, G, 1, 1), (B, Hkv, G, WA_BLOCK, 1))

    def one_block(i):
        q_i = lax.dynamic_index_in_dim(qb, i, axis=1, keepdims=False)
        k_i = lax.dynamic_slice_in_dim(kp, i * WA_BLOCK, n_loc, axis=1)
        v_i = lax.dynamic_slice_in_dim(vp, i * WA_BLOCK, n_loc, axis=1)
        k_pos = i * WA_BLOCK + k_off
        valid = band & ((k_pos >= 0) & (k_pos < S))[None, :]
        s_loc = jnp.einsum('bqkgd,bjkd->bkgqj', q_i, k_i).astype(jnp.float32)
        s_loc = jnp.where(valid, s_loc, NEG_INF)
        s_ctx = jnp.einsum('bqkgd,bckd->bkgqc', q_i, k_ctx).astype(jnp.float32)
        p = jax.nn.softmax(jnp.concatenate([s_loc, s_ctx, sink_col], axis=-1), axis=-1).astype(v.dtype)
        out = (jnp.einsum('bkgqj,bjkd->bqkgd', p[..., :n_loc], v_i)
               + jnp.einsum('bkgqc,bckd->bqkgd', p[..., n_loc:-1], v_ctx))
        return out.reshape(B, WA_BLOCK, Hq * Dh)

    out = lax.map(one_block, jnp.arange(nb))
    return out.transpose(1, 0, 2, 3).reshape(B, S, Hq * Dh)


def split_kv(p):
    o1, o2, o3 = NA_W, 2 * NA_W, 2 * NA_W + WA_KV_W
    return p[..., :o1], p[..., o1:o2], p[..., o2:o3], p[..., o3:]


def setup_inputs(seed: int = 0) -> dict:
    key = jax.random.key(seed)
    ks = jax.random.split(key, 14)
    f32 = jnp.float32

    def nrm(k, shape, scale):
        return jax.random.normal(k, shape, f32) * scale

    return {
        "x": nrm(ks[0], (BATCH, SEQ, D_MODEL), 1.0),
        "c": nrm(ks[1], (BATCH, D_MODEL), 1.0),
        "ctx": nrm(ks[2], (BATCH, CTX_LEN, D_MODEL), 1.0),
        "c_ctx": nrm(ks[3], (D_MODEL,), 1.0),
        "w_ada": nrm(ks[4], (DEPTH, D_MODEL, N_MOD * D_MODEL), 0.5 * D_MODEL ** -0.5),
        "b_ada": nrm(ks[5], (DEPTH, N_MOD * D_MODEL), 0.02),
        "norm_g": 1.0 + nrm(ks[6], (DEPTH, 3, D_MODEL), 0.02),
        "w_ffn_up": nrm(ks[7], (DEPTH, 2, D_MODEL, 2 * D_FF), D_MODEL ** -0.5),
        "w_ffn_down": nrm(ks[8], (DEPTH, 2, D_FF, D_MODEL), D_FF ** -0.5),
        "w_in": nrm(ks[9], (DEPTH, D_MODEL, IN_W), D_MODEL ** -0.5),
        "w_out": nrm(ks[10], (DEPTH, Q_W, D_MODEL), Q_W ** -0.5),
        "qk_norm_g": 1.0 + nrm(ks[11], (DEPTH, 4, HEAD_DIM), 0.02),
        "na_rpb": nrm(ks[12], (DEPTH, NA_HEADS, 2 * NA_KH - 1, 2 * NA_KW - 1), 0.1),
        "wa_sink": nrm(ks[13], (DEPTH, WA_HEADS), 0.5),
    }


def reference(x, c, ctx, c_ctx, w_ada, b_ada, norm_g, w_ffn_up, w_ffn_down, w_in, w_out,
              qk_norm_g, na_rpb, wa_sink):
    S = x.shape[1]
    cos, sin = axial_rope_tables(S)
    h_ctx = ctx
    for l in range(DEPTH):
        last = l == DEPTH - 1
        mod_x = ada_params(c, w_ada[l], b_ada[l])[:, None]
        mod_c = ada_params(c_ctx, w_ada[l], b_ada[l])[None, None]

        x = swiglu_half_step(x, norm_g[l, 0], mod_x, 0, w_ffn_up[l, 0], w_ffn_down[l, 0])
        h_ctx = swiglu_half_step(h_ctx, norm_g[l, 0], mod_c, 0, w_ffn_up[l, 0], w_ffn_down[l, 0])

        hx = modulate(rms_norm(x, norm_g[l, 1]), mod_x[..., 3, :], mod_x[..., 4, :])
        hc = modulate(rms_norm(h_ctx, norm_g[l, 1]), mod_c[..., 3, :], mod_c[..., 4, :])
        px = hx @ w_in[l]
        pc = hc @ (w_in[l, :, Q_W:] if last else w_in[l])
        pc_kv = pc[..., -(IN_W - Q_W):]

        qa = rms_norm(heads(px[..., :NA_W]), qk_norm_g[l, 0])
        qb = rms_norm(heads(px[..., NA_W:Q_W]), qk_norm_g[l, 2])
        ka, va, kb, vb = split_kv(px[..., Q_W:])
        ka = rms_norm(heads(ka), qk_norm_g[l, 1])
        kb = rms_norm(heads(kb), qk_norm_g[l, 3])
        va, vb = heads(va), heads(vb)
        qb = apply_axial_rope(qb, cos, sin)
        kb = apply_axial_rope(kb, cos, sin)

        ka_c, va_c, kb_c, vb_c = split_kv(pc_kv)
        ka_c = rms_norm(heads(ka_c), qk_norm_g[l, 1])
        kb_c = rms_norm(heads(kb_c), qk_norm_g[l, 3])
        va_c, vb_c = heads(va_c), heads(vb_c)

        out_a = neighbourhood_attention(qa, ka, va, ka_c, va_c, na_rpb[l])
        out_b = windowed_gqa_attention(qb, kb, vb, kb_c, vb_c, wa_sink[l])
        x = x + mod_x[..., 5, :] * (jnp.concatenate([out_a, out_b], axis=-1) @ w_out[l])

        if not last:
            qa_c = rms_norm(heads(pc[..., :NA_W]), qk_norm_g[l, 0])
            qb_c = rms_norm(heads(pc[..., NA_W:Q_W]), qk_norm_g[l, 2])
            out_a_c = context_self_attention(qa_c, ka_c, va_c, None)
            out_b_c = context_self_attention(qb_c, kb_c, vb_c, wa_sink[l])
            h_ctx = h_ctx + mod_c[..., 5, :] * (jnp.concatenate([out_a_c, out_b_c], axis=-1) @ w_out[l])

        x = swiglu_half_step(x, norm_g[l, 2], mod_x, 6, w_ffn_up[l, 1], w_ffn_down[l, 1])
        if not last:
            h_ctx = swiglu_half_step(h_ctx, norm_g[l, 2], mod_c, 6, w_ffn_up[l, 1], w_ffn_down[l, 1])
    return x
```

```python
import functools

import numpy as np
import jax
import jax.numpy as jnp
from jax import lax
from jax.experimental import pallas as pl
from jax.experimental.pallas import tpu as pltpu

D_MODEL = 1024
BATCH = 4
SEQ = 8192
DEPTH = 4
CTX_LEN = 256
GRID_W = 64
HEAD_DIM = 64
N_HEADS = D_MODEL // HEAD_DIM
NA_HEADS = N_HEADS // 2
WA_HEADS = N_HEADS - NA_HEADS
WA_KV_HEADS = max(1, WA_HEADS // 4)
NA_KH = 8
NA_KW = 16
WA_WINDOW = 128
WA_BLOCK = 128
D_FF = ((8 * D_MODEL // 3 + 127) // 128) * 128
ROPE_BASE = 10000.0
N_MOD = 9
MACARON_W = 0.5
NORM_EPS = 1e-6
NEG_INF = -1e30
NA_W = NA_HEADS * HEAD_DIM
WA_W = WA_HEADS * HEAD_DIM
WA_KV_W = WA_KV_HEADS * HEAD_DIM
Q_W = NA_W + WA_W
IN_W = Q_W + 2 * NA_W + 2 * WA_KV_W

ROWS = SEQ // GRID_W
TOK = SEQ + CTX_LEN
TM = 256
N_TILES = TOK // TM
CTX_TILE = SEQ // TM
N_COND = 8
CTX_COND = BATCH
NA_RB = 8
NA_QT = NA_RB * GRID_W
WA_SUB = 4
WA_QT = WA_SUB * WA_BLOCK
LANES = 128
MXU_TILE = 256
VMEM_LIMIT = 56 * 1024 * 1024

WA_PERM = np.array([0, 4, 1, 5, 2, 6, 3, 7])

_F32 = jnp.float32
_BF16 = jnp.bfloat16
_NT = (((1,), (1,)), ((), ()))


def _sigmoid(v):
    return 1.0 / (1.0 + jnp.exp(-v))


def _norm_mod(x, g, shift, scale):
    ms = jnp.mean(x * x, axis=-1, keepdims=True)
    h = x * lax.rsqrt(ms + NORM_EPS) * g
    return h * (1.0 + scale) + shift


def _cond_index(b, t):
    return jnp.where(t == CTX_TILE, CTX_COND, b)


ADA_TN = 1152


def _ada_kernel(cond_ref, w_ref, b_ref, o_ref):
    cond = cond_ref[...]
    a = (cond * _sigmoid(cond)).astype(_BF16)
    o_ref[0] = jnp.dot(a, w_ref[0].astype(_BF16), preferred_element_type=_F32) + b_ref[0]


def _ada_params(cond, w_ada, b_ada):
    n_out = N_MOD * D_MODEL
    return pl.pallas_call(
        _ada_kernel,
        grid=(DEPTH, n_out // ADA_TN),
        in_specs=[
            pl.BlockSpec((N_COND, D_MODEL), lambda l, n: (0, 0)),
            pl.BlockSpec((1, D_MODEL, ADA_TN), lambda l, n: (l, 0, n)),
            pl.BlockSpec((1, 1, ADA_TN), lambda l, n: (l, 0, n)),
        ],
        out_specs=pl.BlockSpec((1, N_COND, ADA_TN), lambda l, n: (l, 0, n)),
        out_shape=jax.ShapeDtypeStruct((DEPTH, N_COND, n_out), _F32),
        compiler_params=pltpu.CompilerParams(vmem_limit_bytes=VMEM_LIMIT),
        name="ada_params",
    )(cond, w_ada, b_ada.reshape(DEPTH, 1, n_out))


def _ffn_kernel(x_ref, mod_ref, g_ref, wup_ref, wdn_ref, o_ref, *, j):
    x = x_ref[0]
    mod = mod_ref[0]
    h = _norm_mod(x, g_ref[...], mod[j:j + 1], mod[j + 1:j + 2])
    u = jnp.dot(h.astype(_BF16), wup_ref[...], preferred_element_type=_F32)
    gt = u[:, :D_FF]
    up = u[:, D_FF:]
    a = (gt * _sigmoid(gt) * up).astype(_BF16)
    y = jnp.dot(a, wdn_ref[...], preferred_element_type=_F32)
    o_ref[0] = x + (MACARON_W * mod[j + 2:j + 3]) * y


def _ffn(x_all, mod_l, g, w_up, w_dn, j, n_tiles):
    const = pl.Buffered(1)
    return pl.pallas_call(
        functools.partial(_ffn_kernel, j=j),
        grid=(BATCH, n_tiles),
        in_specs=[
            pl.BlockSpec((1, TM, D_MODEL), lambda b, t: (b, t, 0)),
            pl.BlockSpec((1, N_MOD, D_MODEL), lambda b, t: (_cond_index(b, t), 0, 0)),
            pl.BlockSpec((1, D_MODEL), lambda b, t: (0, 0)),
            pl.BlockSpec((D_MODEL, 2 * D_FF), lambda b, t: (0, 0), pipeline_mode=const),
            pl.BlockSpec((D_FF, D_MODEL), lambda b, t: (0, 0), pipeline_mode=const),
        ],
        out_specs=pl.BlockSpec((1, TM, D_MODEL), lambda b, t: (b, t, 0)),
        out_shape=jax.ShapeDtypeStruct((BATCH, n_tiles * TM, D_MODEL), _F32),
        compiler_params=pltpu.CompilerParams(
            dimension_semantics=("arbitrary", "arbitrary"), vmem_limit_bytes=VMEM_LIMIT),
        name="ffn",
    )(x_all, mod_l, g, w_up, w_dn)


def _head_norm(t, g, bd):
    width = t.shape[1]
    outs = []
    for k in range(0, width, MXU_TILE):
        w = min(MXU_TILE, width - k)
        tc = t[:, k:k + w]
        sq = tc * tc
        hi = sq.astype(_BF16)
        lo = (sq - hi.astype(_F32)).astype(_BF16)
        bdk = bd[:w, :w]
        ms = (jnp.dot(hi, bdk, preferred_element_type=_F32)
              + jnp.dot(lo, bdk, preferred_element_type=_F32))
        outs.append(tc * lax.rsqrt(ms + NORM_EPS))
    out = outs[0] if len(outs) == 1 else jnp.concatenate(outs, axis=-1)
    return out * g


def _rope(t, cos, sin_signed, first_half):
    outs = []
    for k in range(0, t.shape[1], LANES):
        tc = t[:, k:k + LANES]
        partner = jnp.where(first_half,
                            pltpu.roll(tc, LANES - HEAD_DIM // 4, 1),
                            pltpu.roll(tc, HEAD_DIM // 4, 1))
        outs.append(tc * cos + partner * sin_signed)
    return outs[0] if len(outs) == 1 else jnp.concatenate(outs, axis=-1)


def _proj_kernel(x_ref, mod_ref, g_ref, win_ref, g4_ref, bd_ref, cos_ref, sin_ref,
                 qa_ref, qb_ref, ka_ref, va_ref, kb_ref, vb_ref):
    x = x_ref[0]
    mod = mod_ref[0]
    h = _norm_mod(x, g_ref[...], mod[3:4], mod[4:5])
    px = jnp.dot(h.astype(_BF16), win_ref[...], preferred_element_type=_F32)
    bd = bd_ref[...]
    g4 = g4_ref[...]
    cos = cos_ref[...]
    sin = sin_ref[...]
    lane = lax.broadcasted_iota(jnp.int32, (1, LANES), 1)
    first_half = (lane % (HEAD_DIM // 2)) < (HEAD_DIM // 4)
    q_scale = HEAD_DIM ** -0.5

    o1, o2, o3 = Q_W, Q_W + NA_W, Q_W + 2 * NA_W
    o4 = o3 + WA_KV_W
    qa = _head_norm(px[:, :NA_W], g4[0:1, :NA_W], bd)
    qa_ref[0] = (qa * q_scale).astype(_BF16)
    qb = _rope(_head_norm(px[:, NA_W:Q_W], g4[2:3, :WA_W], bd), cos, sin, first_half)
    qb_ref[0] = (qb * q_scale).astype(_BF16)
    ka = _head_norm(px[:, o1:o2], g4[1:2, :NA_W], bd)
    ka_ref[0] = ka.astype(_BF16)
    va_ref[0] = px[:, o2:o3].astype(_BF16)
    kb = _rope(_head_norm(px[:, o3:o4], g4[3:4, :WA_KV_W], bd), cos, sin, first_half)
    kb_ref[0] = kb.astype(_BF16)
    vb_ref[0] = px[:, o4:].astype(_BF16)


def _proj(x_all, mod_l, g, w_in, g4, bd, cos_t, sin_t):
    const = pl.Buffered(1)
    tok = lambda w: pl.BlockSpec((1, TM, w), lambda b, t: (b, t, 0))
    shp = lambda w: jax.ShapeDtypeStruct((BATCH, TOK, w), _BF16)
    return pl.pallas_call(
        _proj_kernel,
        grid=(BATCH, N_TILES),
        in_specs=[
            tok(D_MODEL),
            pl.BlockSpec((1, N_MOD, D_MODEL), lambda b, t: (_cond_index(b, t), 0, 0)),
            pl.BlockSpec((1, D_MODEL), lambda b, t: (0, 0)),
            pl.BlockSpec((D_MODEL, IN_W), lambda b, t: (0, 0), pipeline_mode=const),
            pl.BlockSpec((4, NA_W), lambda b, t: (0, 0)),
            pl.BlockSpec((MXU_TILE, MXU_TILE), lambda b, t: (0, 0)),
            pl.BlockSpec((TM, LANES), lambda b, t: (t, 0)),
            pl.BlockSpec((TM, LANES), lambda b, t: (t, 0)),
        ],
        out_specs=[tok(NA_W), tok(WA_W), tok(NA_W), tok(NA_W), tok(WA_KV_W), tok(WA_KV_W)],
        out_shape=[shp(NA_W), shp(WA_W), shp(NA_W), shp(NA_W), shp(WA_KV_W), shp(WA_KV_W)],
        compiler_params=pltpu.CompilerParams(
            dimension_semantics=("arbitrary", "arbitrary"), vmem_limit_bytes=VMEM_LIMIT),
        name="qkv_proj",
    )(x_all, mod_l, g, w_in, g4, bd, cos_t, sin_t)


def _lane_masks():
    lane = lax.broadcasted_iota(jnp.int32, (1, LANES), 1)
    lo = lane < HEAD_DIM
    return lo, lo.astype(_BF16), (~lo).astype(_BF16)


def _softmax_pv(scores, values, extra_logit=None):
    m = jnp.max(scores[0], axis=-1, keepdims=True)
    for s in scores[1:]:
        m = jnp.maximum(m, jnp.max(s, axis=-1, keepdims=True))
    if extra_logit is not None:
        m = jnp.maximum(m, extra_logit)
    denom = None if extra_logit is None else jnp.exp(extra_logit - m)
    acc = None
    for s, v in zip(scores, values):
        p = jnp.exp(s - m)
        ps = jnp.sum(p, axis=-1, keepdims=True)
        denom = ps if denom is None else denom + ps
        pv = jnp.dot(p.astype(_BF16), v, preferred_element_type=_F32)
        acc = pv if acc is None else acc + pv
    return acc / denom


def _na_kernel(q_ref, kp_ref, kc_ref, kn_ref, vp_ref, vc_ref, vn_ref, kx_ref, vx_ref,
               tbl_ref, o_ref, kbuf, vbuf):
    rb = pl.program_id(1)
    kbuf[0:NA_QT] = kp_ref[0]
    kbuf[NA_QT:2 * NA_QT] = kc_ref[0]
    kbuf[2 * NA_QT:3 * NA_QT] = kn_ref[0]
    vbuf[0:NA_QT] = vp_ref[0]
    vbuf[NA_QT:2 * NA_QT] = vc_ref[0]
    vbuf[2 * NA_QT:3 * NA_QT] = vn_ref[0]
    lo, m_lo, m_hi = _lane_masks()
    n_win = NA_KH * GRID_W

    def row(j, carry):
        r = rb * NA_RB + j
        r0 = jnp.clip(r - NA_KH // 2, 0, ROWS - NA_KH)
        dr0 = r0 - r + (NA_KH - 1)
        off = pl.multiple_of((r0 - rb * NA_RB + NA_RB) * GRID_W, GRID_W)
        qoff = pl.multiple_of(j * GRID_W, GRID_W)
        qrow = q_ref[0, pl.ds(qoff, GRID_W), :]
        outs = []
        for p in range(NA_HEADS // 2):
            cols = slice(p * LANES, (p + 1) * LANES)
            qp = qrow[:, cols]
            q2 = jnp.concatenate([qp * m_lo, qp * m_hi], axis=0)
            kw = kbuf[pl.ds(off, n_win), cols]
            vw = vbuf[pl.ds(off, n_win), cols]
            s_nb = lax.dot_general(q2, kw, _NT, preferred_element_type=_F32)
            bias = jnp.concatenate([
                jnp.concatenate([tbl_ref[2 * p + hh, pl.ds(dr0 + 2 * k, 1)][0]
                                 for k in range(NA_KH // 2)], axis=1)
                for hh in range(2)], axis=0)
            s_nb = s_nb + bias
            s_cx = lax.dot_general(q2, kx_ref[0, :, cols], _NT, preferred_element_type=_F32)
            o2 = _softmax_pv([s_nb, s_cx], [vw, vx_ref[0, :, cols]])
            outs.append(jnp.where(lo, o2[:GRID_W], o2[GRID_W:]))
        o_ref[0, pl.ds(qoff, GRID_W), :] = jnp.concatenate(outs, axis=1).astype(_BF16)
        return carry

    lax.fori_loop(0, NA_RB, row, 0)


def _na_attention(qa, ka, va, tbl):
    n_rb = ROWS // NA_RB
    blk = lambda f: pl.BlockSpec((1, NA_QT, NA_W), f)
    prev = lambda b, i: (b, jnp.maximum(i - 1, 0), 0)
    cur = lambda b, i: (b, i, 0)
    nxt = lambda b, i: (b, jnp.minimum(i + 1, n_rb - 1), 0)
    ctx = pl.BlockSpec((1, CTX_LEN, NA_W), lambda b, i: (b, SEQ // CTX_LEN, 0))
    return pl.pallas_call(
        _na_kernel,
        grid=(BATCH, n_rb),
        in_specs=[blk(cur), blk(prev), blk(cur), blk(nxt), blk(prev), blk(cur), blk(nxt),
                  ctx, ctx,
                  pl.BlockSpec(tbl.shape, lambda b, i: (0, 0, 0, 0))],
        out_specs=blk(cur),
        out_shape=jax.ShapeDtypeStruct((BATCH, TOK, NA_W), _BF16),
        scratch_shapes=[pltpu.VMEM((3 * NA_QT, NA_W), _BF16),
                        pltpu.VMEM((3 * NA_QT, NA_W), _BF16)],
        compiler_params=pltpu.CompilerParams(
            dimension_semantics=("arbitrary", "arbitrary"), vmem_limit_bytes=VMEM_LIMIT),
        name="na_attention",
    )(qa, ka, ka, ka, va, va, va, ka, va, tbl)


def _stack_heads(q, m_lo, m_hi):
    parts = []
    for pos in range(WA_HEADS):
        qp = q[:, (pos // 2) * LANES:(pos // 2 + 1) * LANES]
        parts.append(qp * (m_lo if pos % 2 == 0 else m_hi))
    return jnp.concatenate(parts, axis=0)


def _unstack_heads(o, rows, lo):
    outs = []
    for g in range(WA_HEADS // 2):
        outs.append(jnp.where(lo, o[(2 * g) * rows:(2 * g + 1) * rows],
                              o[(2 * g + 1) * rows:(2 * g + 2) * rows]))
    return jnp.concatenate(outs, axis=1)


def _sink_column(sink_ref, rows):
    return jnp.concatenate(
        [jnp.full((rows, 1), sink_ref[pos], _F32) for pos in range(WA_HEADS)], axis=0)


def _wa_kernel(sink_ref, q_ref, kp_ref, kc_ref, kn_ref, vp_ref, vc_ref, vn_ref, kx_ref, vx_ref,
               o_ref, kbuf, vbuf):
    i = pl.program_id(1)
    kbuf[0:WA_BLOCK] = kp_ref[0]
    kbuf[WA_BLOCK:WA_BLOCK + WA_QT] = kc_ref[0]
    kbuf[WA_BLOCK + WA_QT:] = kn_ref[0]
    vbuf[0:WA_BLOCK] = vp_ref[0]
    vbuf[WA_BLOCK:WA_BLOCK + WA_QT] = vc_ref[0]
    vbuf[WA_BLOCK + WA_QT:] = vn_ref[0]
    lo, m_lo, m_hi = _lane_masks()
    n_loc = 3 * WA_BLOCK
    q_off = lax.broadcasted_iota(jnp.int32, (WA_BLOCK, n_loc), 0)
    k_off = lax.broadcasted_iota(jnp.int32, (WA_BLOCK, n_loc), 1) - WA_BLOCK
    band = jnp.abs(k_off - q_off) <= WA_WINDOW
    sink_col = _sink_column(sink_ref, WA_BLOCK)
    kx = kx_ref[0]
    vx = vx_ref[0]
    for t in range(WA_SUB):
        k_pos = (i * WA_SUB + t) * WA_BLOCK + k_off
        valid = band & (k_pos >= 0) & (k_pos < SEQ)
        q8 = _stack_heads(q_ref[0, t * WA_BLOCK:(t + 1) * WA_BLOCK, :], m_lo, m_hi)
        kw = kbuf[t * WA_BLOCK:t * WA_BLOCK + n_loc, :]
        vw = vbuf[t * WA_BLOCK:t * WA_BLOCK + n_loc, :]
        s_loc = lax.dot_general(q8, kw, _NT, preferred_element_type=_F32)
        s_loc = jnp.where(valid[None], s_loc.reshape(WA_HEADS, WA_BLOCK, n_loc), NEG_INF)
        s_loc = s_loc.reshape(WA_HEADS * WA_BLOCK, n_loc)
        s_cx = lax.dot_general(q8, kx, _NT, preferred_element_type=_F32)
        o8 = _softmax_pv([s_loc, s_cx], [vw, vx], extra_logit=sink_col)
        o_ref[0, t * WA_BLOCK:(t + 1) * WA_BLOCK, :] = (
            _unstack_heads(o8, WA_BLOCK, lo).astype(_BF16))


def _wa_attention(qb, kb, vb, sink):
    n_blk = SEQ // WA_BLOCK
    qspec = pl.BlockSpec((1, WA_QT, WA_W), lambda b, i: (b, i, 0))
    cur = pl.BlockSpec((1, WA_QT, WA_KV_W), lambda b, i: (b, i, 0))
    prev = pl.BlockSpec((1, WA_BLOCK, WA_KV_W),
                        lambda b, i: (b, jnp.maximum(i * WA_SUB - 1, 0), 0))
    nxt = pl.BlockSpec((1, WA_BLOCK, WA_KV_W),
                       lambda b, i: (b, jnp.minimum((i + 1) * WA_SUB, n_blk - 1), 0))
    ctx = pl.BlockSpec((1, CTX_LEN, WA_KV_W), lambda b, i: (b, SEQ // CTX_LEN, 0))
    return pl.pallas_call(
        _wa_kernel,
        grid=(BATCH, SEQ // WA_QT),
        in_specs=[pl.BlockSpec(memory_space=pltpu.SMEM),
                  qspec, prev, cur, nxt, prev, cur, nxt, ctx, ctx],
        out_specs=qspec,
        out_shape=jax.ShapeDtypeStruct((BATCH, TOK, WA_W), _BF16),
        scratch_shapes=[pltpu.VMEM((WA_QT + 2 * WA_BLOCK, WA_KV_W), _BF16),
                        pltpu.VMEM((WA_QT + 2 * WA_BLOCK, WA_KV_W), _BF16)],
        compiler_params=pltpu.CompilerParams(
            dimension_semantics=("arbitrary", "arbitrary"), vmem_limit_bytes=VMEM_LIMIT),
        name="wa_attention",
    )(sink, qb, kb, kb, kb, vb, vb, vb, kb, vb)


def _ctx_kernel(sink_ref, qa_ref, ka_ref, va_ref, qb_ref, kb_ref, vb_ref, oa_in, ob_in,
                oa_ref, ob_ref):
    del oa_in, ob_in
    lo, m_lo, m_hi = _lane_masks()
    qa = qa_ref[0]
    outs = []
    for p in range(NA_HEADS // 2):
        cols = slice(p * LANES, (p + 1) * LANES)
        qp = qa[:, cols]
        q2 = jnp.concatenate([qp * m_lo, qp * m_hi], axis=0)
        s = lax.dot_general(q2, ka_ref[0, :, cols], _NT, preferred_element_type=_F32)
        o2 = _softmax_pv([s], [va_ref[0, :, cols]])
        outs.append(jnp.where(lo, o2[:CTX_LEN], o2[CTX_LEN:]))
    oa_ref[0] = jnp.concatenate(outs, axis=1).astype(_BF16)

    q8 = _stack_heads(qb_ref[0], m_lo, m_hi)
    s = lax.dot_general(q8, kb_ref[0], _NT, preferred_element_type=_F32)
    o8 = _softmax_pv([s], [vb_ref[0]], extra_logit=_sink_column(sink_ref, CTX_LEN))
    ob_ref[0] = _unstack_heads(o8, CTX_LEN, lo).astype(_BF16)


def _ctx_attention(qa, ka, va, qb, kb, vb, sink, attn_a, attn_b):
    ctx = lambda w: pl.BlockSpec((1, CTX_LEN, w), lambda b: (b, SEQ // CTX_LEN, 0))
    return pl.pallas_call(
        _ctx_kernel,
        grid=(BATCH,),
        in_specs=[pl.BlockSpec(memory_space=pltpu.SMEM),
                  ctx(NA_W), ctx(NA_W), ctx(NA_W), ctx(WA_W), ctx(WA_KV_W), ctx(WA_KV_W),
                  pl.BlockSpec(memory_space=pl.ANY), pl.BlockSpec(memory_space=pl.ANY)],
        out_specs=[ctx(NA_W), ctx(WA_W)],
        out_shape=[jax.ShapeDtypeStruct(attn_a.shape, _BF16),
                   jax.ShapeDtypeStruct(attn_b.shape, _BF16)],
        input_output_aliases={7: 0, 8: 1},
        compiler_params=pltpu.CompilerParams(
            dimension_semantics=("arbitrary",), vmem_limit_bytes=VMEM_LIMIT),
        name="ctx_attention",
    )(sink, qa, ka, va, qb, kb, vb, attn_a, attn_b)


def _outproj_kernel(x_ref, mod_ref, aa_ref, ab_ref, wo_ref, o_ref):
    y = (jnp.dot(aa_ref[0], wo_ref[:NA_W], preferred_element_type=_F32)
         + jnp.dot(ab_ref[0], wo_ref[NA_W:], preferred_element_type=_F32))
    o_ref[0] = x_ref[0] + mod_ref[0][5:6] * y


def _outproj(x_all, mod_l, attn_a, attn_b, w_out):
    tok = lambda w: pl.BlockSpec((1, TM, w), lambda b, t: (b, t, 0))
    return pl.pallas_call(
        _outproj_kernel,
        grid=(BATCH, N_TILES),
        in_specs=[
            tok(D_MODEL),
            pl.BlockSpec((1, N_MOD, D_MODEL), lambda b, t: (_cond_index(b, t), 0, 0)),
            tok(NA_W), tok(WA_W),
            pl.BlockSpec((Q_W, D_MODEL), lambda b, t: (0, 0)),
        ],
        out_specs=tok(D_MODEL),
        out_shape=jax.ShapeDtypeStruct((BATCH, TOK, D_MODEL), _F32),
        compiler_params=pltpu.CompilerParams(
            dimension_semantics=("arbitrary", "arbitrary"), vmem_limit_bytes=VMEM_LIMIT),
        name="out_proj",
    )(x_all, mod_l, attn_a, attn_b, w_out)


def _rope_tables():
    rot = HEAD_DIM // 2
    inv_freq = ROPE_BASE ** (-jnp.arange(0, rot, 2, dtype=_F32) / rot)
    t = jnp.arange(SEQ)
    row = (t // GRID_W).astype(_F32)
    col = (t % GRID_W).astype(_F32)
    ang = jnp.stack([row[:, None] * inv_freq, col[:, None] * inv_freq], axis=1)
    d = np.arange(LANES) % HEAD_DIM
    axis, half, freq = d // rot, (d % rot) // (rot // 2), d % (rot // 2)
    ang_l = ang[:, axis, freq]
    sign = jnp.asarray(np.where(half == 0, -1.0, 1.0), _F32)
    cos_t = jnp.concatenate([jnp.cos(ang_l), jnp.ones((CTX_LEN, LANES), _F32)], axis=0)
    sin_t = jnp.concatenate([jnp.sin(ang_l) * sign, jnp.zeros((CTX_LEN, LANES), _F32)], axis=0)
    return cos_t, sin_t


def _na_bias_table(rpb):
    col = np.arange(GRID_W)
    c0 = np.clip(col - NA_KW // 2, 0, GRID_W - NA_KW)
    in_win = (col[None, :] >= c0[:, None]) & (col[None, :] < c0[:, None] + NA_KW)
    dc = np.clip(col[None, :] - col[:, None] + (NA_KW - 1), 0, 2 * NA_KW - 2)
    full = jnp.where(in_win[None, None], rpb.astype(_F32)[:, :, dc], NEG_INF)
    return jnp.concatenate([full[:, :-1], full[:, 1:]], axis=-1)


def kernel(x, c, ctx, c_ctx, w_ada, b_ada, norm_g, w_ffn_up, w_ffn_down, w_in, w_out,
           qk_norm_g, na_rpb, wa_sink):
    q_cols = np.arange(IN_W)
    q_cols[NA_W:Q_W] = NA_W + (WA_PERM[:, None] * HEAD_DIM + np.arange(HEAD_DIM)[None]).reshape(-1)
    o_rows = q_cols[:Q_W]
    w_up_b = w_ffn_up.astype(_BF16)
    w_dn_b = w_ffn_down.astype(_BF16)
    w_in_b = w_in[:, :, q_cols].astype(_BF16)
    w_out_b = w_out[:, o_rows, :].astype(_BF16)
    sink_p = wa_sink[:, WA_PERM].astype(_F32)
    g4 = jnp.tile(qk_norm_g.astype(_F32), (1, 1, NA_W // HEAD_DIM))
    blk = np.arange(MXU_TILE) // HEAD_DIM
    bd = jnp.asarray((blk[:, None] == blk[None, :]) / HEAD_DIM, _BF16)
    cos_t, sin_t = _rope_tables()

    cond = jnp.zeros((N_COND, D_MODEL), _F32).at[:BATCH].set(c).at[CTX_COND].set(c_ctx)
    mods = _ada_params(cond, w_ada, b_ada).reshape(DEPTH, N_COND, N_MOD, D_MODEL)

    x_all = jnp.concatenate([x, ctx], axis=1)
    for l in range(DEPTH):
        last = l == DEPTH - 1
        mod_l = mods[l]
        x_all = _ffn(x_all, mod_l, norm_g[l, 0:1], w_up_b[l, 0], w_dn_b[l, 0], 0, N_TILES)
        qa, qb, ka, va, kb, vb = _proj(x_all, mod_l, norm_g[l, 1:2], w_in_b[l], g4[l], bd,
                                       cos_t, sin_t)
        attn_a = _na_attention(qa, ka, va, _na_bias_table(na_rpb[l]))
        attn_b = _wa_attention(qb, kb, vb, sink_p[l])
        attn_a, attn_b = _ctx_attention(qa, ka, va, qb, kb, vb, sink_p[l], attn_a, attn_b)
        x_all = _outproj(x_all, mod_l, attn_a, attn_b, w_out_b[l])
        x_all = _ffn(x_all, mod_l, norm_g[l, 2:3], w_up_b[l, 1], w_dn_b[l, 1], 6,
                     CTX_TILE if last else N_TILES)
    return x_all
```

```python
import functools

import numpy as np
import jax
import jax.numpy as jnp
from jax import lax
from jax.experimental import pallas as pl
from jax.experimental.pallas import tpu as pltpu

D_MODEL = 1024
BATCH = 4
SEQ = 8192
DEPTH = 4
CTX_LEN = 256
GRID_W = 64
HEAD_DIM = 64
N_HEADS = D_MODEL // HEAD_DIM
NA_HEADS = N_HEADS // 2
WA_HEADS = N_HEADS - NA_HEADS
WA_KV_HEADS = max(1, WA_HEADS // 4)
NA_KH = 8
NA_KW = 16
WA_WINDOW = 128
WA_BLOCK = 128
D_FF = ((8 * D_MODEL // 3 + 127) // 128) * 128
ROPE_BASE = 10000.0
N_MOD = 9
MACARON_W = 0.5
NORM_EPS = 1e-6
NEG_INF = -1e30
NA_W = NA_HEADS * HEAD_DIM
WA_W = WA_HEADS * HEAD_DIM
WA_KV_W = WA_KV_HEADS * HEAD_DIM
Q_W = NA_W + WA_W
IN_W = Q_W + 2 * NA_W + 2 * WA_KV_W

ROWS = SEQ // GRID_W
TOK = SEQ + CTX_LEN
TM = 256
N_TILES = TOK // TM
CTX_TILE = SEQ // TM
N_COND = 8
CTX_COND = BATCH
NA_RB = 8
NA_QT = NA_RB * GRID_W
WA_SUB = 4
WA_QT = WA_SUB * WA_BLOCK
NA_DEPTH = 3
WA_DEPTH = 2
LANES = 128
MXU_TILE = 256
VMEM_LIMIT = 56 * 1024 * 1024

WA_PERM = np.array([0, 4, 1, 5, 2, 6, 3, 7])

_F32 = jnp.float32
_BF16 = jnp.bfloat16
_NT = (((1,), (1,)), ((), ()))


def _sigmoid(v):
    return 1.0 / (1.0 + jnp.exp(-v))


def _norm_mod(x, g, shift, scale):
    ms = jnp.mean(x * x, axis=-1, keepdims=True)
    h = x * lax.rsqrt(ms + NORM_EPS) * g
    return h * (1.0 + scale) + shift


def _cond_index(b, t):
    return jnp.where(t == CTX_TILE, CTX_COND, b)


ADA_TN = 1152


def _ada_kernel(cond_ref, w_ref, b_ref, o_ref):
    cond = cond_ref[...]
    a = (cond * _sigmoid(cond)).astype(_BF16)
    o_ref[0] = jnp.dot(a, w_ref[0].astype(_BF16), preferred_element_type=_F32) + b_ref[0]


def _ada_params(cond, w_ada, b_ada):
    n_out = N_MOD * D_MODEL
    return pl.pallas_call(
        _ada_kernel,
        grid=(DEPTH, n_out // ADA_TN),
        in_specs=[
            pl.BlockSpec((N_COND, D_MODEL), lambda l, n: (0, 0)),
            pl.BlockSpec((1, D_MODEL, ADA_TN), lambda l, n: (l, 0, n)),
            pl.BlockSpec((1, 1, ADA_TN), lambda l, n: (l, 0, n)),
        ],
        out_specs=pl.BlockSpec((1, N_COND, ADA_TN), lambda l, n: (l, 0, n)),
        out_shape=jax.ShapeDtypeStruct((DEPTH, N_COND, n_out), _F32),
        compiler_params=pltpu.CompilerParams(vmem_limit_bytes=VMEM_LIMIT),
        name="ada_params",
    )(cond, w_ada, b_ada.reshape(DEPTH, 1, n_out))


def _ffn_kernel(x_ref, mod_ref, g_ref, wup_ref, wdn_ref, o_ref, *, j):
    x = x_ref[0]
    mod = mod_ref[0]
    h = _norm_mod(x, g_ref[...], mod[j:j + 1], mod[j + 1:j + 2])
    u = jnp.dot(h.astype(_BF16), wup_ref[...], preferred_element_type=_F32)
    gt = u[:, :D_FF]
    up = u[:, D_FF:]
    a = (gt * _sigmoid(gt) * up).astype(_BF16)
    y = jnp.dot(a, wdn_ref[...], preferred_element_type=_F32)
    o_ref[0] = x + (MACARON_W * mod[j + 2:j + 3]) * y


def _ffn(x_all, mod_l, g, w_up, w_dn, j, n_tiles):
    const = pl.Buffered(1)
    return pl.pallas_call(
        functools.partial(_ffn_kernel, j=j),
        grid=(BATCH, n_tiles),
        in_specs=[
            pl.BlockSpec((1, TM, D_MODEL), lambda b, t: (b, t, 0)),
            pl.BlockSpec((1, N_MOD, D_MODEL), lambda b, t: (_cond_index(b, t), 0, 0)),
            pl.BlockSpec((1, D_MODEL), lambda b, t: (0, 0)),
            pl.BlockSpec((D_MODEL, 2 * D_FF), lambda b, t: (0, 0), pipeline_mode=const),
            pl.BlockSpec((D_FF, D_MODEL), lambda b, t: (0, 0), pipeline_mode=const),
        ],
        out_specs=pl.BlockSpec((1, TM, D_MODEL), lambda b, t: (b, t, 0)),
        out_shape=jax.ShapeDtypeStruct((BATCH, n_tiles * TM, D_MODEL), _F32),
        compiler_params=pltpu.CompilerParams(
            dimension_semantics=("arbitrary", "arbitrary"), vmem_limit_bytes=VMEM_LIMIT),
        name="ffn",
    )(x_all, mod_l, g, w_up, w_dn)


def _head_norm(t, g, bd):
    width = t.shape[1]
    outs = []
    for k in range(0, width, MXU_TILE):
        w = min(MXU_TILE, width - k)
        tc = t[:, k:k + w]
        sq = tc * tc
        hi = sq.astype(_BF16)
        lo = (sq - hi.astype(_F32)).astype(_BF16)
        bdk = bd[:w, :w]
        ms = (jnp.dot(hi, bdk, preferred_element_type=_F32)
              + jnp.dot(lo, bdk, preferred_element_type=_F32))
        outs.append(tc * lax.rsqrt(ms + NORM_EPS))
    out = outs[0] if len(outs) == 1 else jnp.concatenate(outs, axis=-1)
    return out * g


def _rope(t, cos, sin_signed, first_half):
    outs = []
    for k in range(0, t.shape[1], LANES):
        tc = t[:, k:k + LANES]
        partner = jnp.where(first_half,
                            pltpu.roll(tc, LANES - HEAD_DIM // 4, 1),
                            pltpu.roll(tc, HEAD_DIM // 4, 1))
        outs.append(tc * cos + partner * sin_signed)
    return outs[0] if len(outs) == 1 else jnp.concatenate(outs, axis=-1)


def _proj_kernel(x_ref, mod_ref, g_ref, win_ref, g4_ref, bd_ref, cos_ref, sin_ref,
                 qa_ref, qb_ref, ka_ref, va_ref, kb_ref, vb_ref):
    x = x_ref[0]
    mod = mod_ref[0]
    h = _norm_mod(x, g_ref[...], mod[3:4], mod[4:5])
    px = jnp.dot(h.astype(_BF16), win_ref[...], preferred_element_type=_F32)
    bd = bd_ref[...]
    g4 = g4_ref[...]
    cos = cos_ref[...]
    sin = sin_ref[...]
    lane = lax.broadcasted_iota(jnp.int32, (1, LANES), 1)
    first_half = (lane % (HEAD_DIM // 2)) < (HEAD_DIM // 4)
    q_scale = HEAD_DIM ** -0.5

    o1, o2, o3 = Q_W, Q_W + NA_W, Q_W + 2 * NA_W
    o4 = o3 + WA_KV_W
    qa = _head_norm(px[:, :NA_W], g4[0:1, :NA_W], bd)
    qa_ref[0] = (qa * q_scale).astype(_BF16)
    qb = _rope(_head_norm(px[:, NA_W:Q_W], g4[2:3, :WA_W], bd), cos, sin, first_half)
    qb_ref[0] = (qb * q_scale).astype(_BF16)
    ka = _head_norm(px[:, o1:o2], g4[1:2, :NA_W], bd)
    ka_ref[0] = ka.astype(_BF16)
    va_ref[0] = px[:, o2:o3].astype(_BF16)
    kb = _rope(_head_norm(px[:, o3:o4], g4[3:4, :WA_KV_W], bd), cos, sin, first_half)
    kb_ref[0] = kb.astype(_BF16)
    vb_ref[0] = px[:, o4:].astype(_BF16)


def _proj(x_all, mod_l, g, w_in, g4, bd, cos_t, sin_t):
    const = pl.Buffered(1)
    tok = lambda w: pl.BlockSpec((1, TM, w), lambda b, t: (b, t, 0))
    shp = lambda w: jax.ShapeDtypeStruct((BATCH, TOK, w), _BF16)
    return pl.pallas_call(
        _proj_kernel,
        grid=(BATCH, N_TILES),
        in_specs=[
            tok(D_MODEL),
            pl.BlockSpec((1, N_MOD, D_MODEL), lambda b, t: (_cond_index(b, t), 0, 0)),
            pl.BlockSpec((1, D_MODEL), lambda b, t: (0, 0)),
            pl.BlockSpec((D_MODEL, IN_W), lambda b, t: (0, 0), pipeline_mode=const),
            pl.BlockSpec((4, NA_W), lambda b, t: (0, 0)),
            pl.BlockSpec((MXU_TILE, MXU_TILE), lambda b, t: (0, 0)),
            pl.BlockSpec((TM, LANES), lambda b, t: (t, 0)),
            pl.BlockSpec((TM, LANES), lambda b, t: (t, 0)),
        ],
        out_specs=[tok(NA_W), tok(WA_W), tok(NA_W), tok(NA_W), tok(WA_KV_W), tok(WA_KV_W)],
        out_shape=[shp(NA_W), shp(WA_W), shp(NA_W), shp(NA_W), shp(WA_KV_W), shp(WA_KV_W)],
        compiler_params=pltpu.CompilerParams(
            dimension_semantics=("arbitrary", "arbitrary"), vmem_limit_bytes=VMEM_LIMIT),
        name="qkv_proj",
    )(x_all, mod_l, g, w_in, g4, bd, cos_t, sin_t)


def _lane_masks():
    lane = lax.broadcasted_iota(jnp.int32, (1, LANES), 1)
    lo = lane < HEAD_DIM
    return lo, lo.astype(_BF16), (~lo).astype(_BF16)


def _softmax_pv(scores, values, extra_logit=None):
    m = jnp.max(_fold_lane_groups(scores, jnp.maximum), axis=-1, keepdims=True)
    if extra_logit is not None:
        m = jnp.maximum(m, extra_logit)
    probs = [jnp.exp(s - m) for s in scores]
    denom = jnp.sum(_fold_lane_groups(probs, jnp.add), axis=-1, keepdims=True)
    if extra_logit is not None:
        denom = denom + jnp.exp(extra_logit - m)
    acc = None
    for p, v in zip(probs, values):
        pv = jnp.dot(p.astype(_BF16), v, preferred_element_type=_F32)
        acc = pv if acc is None else acc + pv
    return acc / denom


def _fold_lane_groups(blocks, op):
    acc = None
    for s in blocks:
        for k in range(0, s.shape[1], LANES):
            c = s[:, k:k + LANES]
            acc = c if acc is None else op(acc, c)
    return acc


def _emit_pipelined(n_units, depth, start, finish):
    pending = []
    for u in range(n_units + depth):
        if u < n_units:
            pending.append(start(u))
        if u >= depth:
            finish(u - depth, pending.pop(0))


def _na_kernel(q_ref, kp_ref, kc_ref, kn_ref, vp_ref, vc_ref, vn_ref, kx_ref, vx_ref,
               tbl_ref, o_ref, kbuf, vbuf):
    rb = pl.program_id(1)
    kbuf[0:NA_QT] = kp_ref[0]
    kbuf[NA_QT:2 * NA_QT] = kc_ref[0]
    kbuf[2 * NA_QT:3 * NA_QT] = kn_ref[0]
    vbuf[0:NA_QT] = vp_ref[0]
    vbuf[NA_QT:2 * NA_QT] = vc_ref[0]
    vbuf[2 * NA_QT:3 * NA_QT] = vn_ref[0]
    lo, m_lo, m_hi = _lane_masks()
    n_win = NA_KH * GRID_W

    n_pairs = NA_HEADS // 2

    def geometry(u):
        j, p = divmod(u, n_pairs)
        r = rb * NA_RB + j
        r0 = jnp.clip(r - NA_KH // 2, 0, ROWS - NA_KH)
        dr0 = r0 - r + (NA_KH - 1)
        off = pl.multiple_of((r0 - rb * NA_RB + NA_RB) * GRID_W, GRID_W)
        return j, p, dr0, off, slice(p * LANES, (p + 1) * LANES)

    def scores(u):
        j, _, _, off, cols = geometry(u)
        qp = q_ref[0, j * GRID_W:(j + 1) * GRID_W, cols]
        q2 = jnp.concatenate([qp * m_lo, qp * m_hi], axis=0)
        s_nb = lax.dot_general(q2, kbuf[pl.ds(off, n_win), cols], _NT,
                               preferred_element_type=_F32)
        s_cx = lax.dot_general(q2, kx_ref[0, :, cols], _NT, preferred_element_type=_F32)
        return s_nb, s_cx

    def finish(u, s):
        j, p, dr0, off, cols = geometry(u)
        s_nb, s_cx = s
        bias = jnp.concatenate([
            jnp.concatenate([tbl_ref[2 * p + hh, pl.ds(dr0 + 2 * k, 1)][0]
                             for k in range(NA_KH // 2)], axis=1)
            for hh in range(2)], axis=0)
        o2 = _softmax_pv([s_nb + bias, s_cx],
                         [vbuf[pl.ds(off, n_win), cols], vx_ref[0, :, cols]])
        o_ref[0, j * GRID_W:(j + 1) * GRID_W, cols] = (
            jnp.where(lo, o2[:GRID_W], o2[GRID_W:]).astype(_BF16))

    _emit_pipelined(NA_RB * n_pairs, NA_DEPTH, scores, finish)


def _na_attention(qa, ka, va, tbl):
    n_rb = ROWS // NA_RB
    blk = lambda f: pl.BlockSpec((1, NA_QT, NA_W), f)
    prev = lambda b, i: (b, jnp.maximum(i - 1, 0), 0)
    cur = lambda b, i: (b, i, 0)
    nxt = lambda b, i: (b, jnp.minimum(i + 1, n_rb - 1), 0)
    ctx = pl.BlockSpec((1, CTX_LEN, NA_W), lambda b, i: (b, SEQ // CTX_LEN, 0))
    return pl.pallas_call(
        _na_kernel,
        grid=(BATCH, n_rb),
        in_specs=[blk(cur), blk(prev), blk(cur), blk(nxt), blk(prev), blk(cur), blk(nxt),
                  ctx, ctx,
                  pl.BlockSpec(tbl.shape, lambda b, i: (0, 0, 0, 0))],
        out_specs=blk(cur),
        out_shape=jax.ShapeDtypeStruct((BATCH, TOK, NA_W), _BF16),
        scratch_shapes=[pltpu.VMEM((3 * NA_QT, NA_W), _BF16),
                        pltpu.VMEM((3 * NA_QT, NA_W), _BF16)],
        compiler_params=pltpu.CompilerParams(
            dimension_semantics=("arbitrary", "arbitrary"), vmem_limit_bytes=VMEM_LIMIT),
        name="na_attention",
    )(qa, ka, ka, ka, va, va, va, ka, va, tbl)


def _stack_heads(q, m_lo, m_hi):
    parts = []
    for pos in range(WA_HEADS):
        qp = q[:, (pos // 2) * LANES:(pos // 2 + 1) * LANES]
        parts.append(qp * (m_lo if pos % 2 == 0 else m_hi))
    return jnp.concatenate(parts, axis=0)


def _unstack_heads(o, rows, lo):
    outs = []
    for g in range(WA_HEADS // 2):
        outs.append(jnp.where(lo, o[(2 * g) * rows:(2 * g + 1) * rows],
                              o[(2 * g + 1) * rows:(2 * g + 2) * rows]))
    return jnp.concatenate(outs, axis=1)


def _sink_column(sink_ref, rows):
    return jnp.concatenate(
        [jnp.full((rows, 1), sink_ref[pos], _F32) for pos in range(WA_HEADS)], axis=0)


def _wa_kernel(sink_ref, q_ref, kp_ref, kc_ref, kn_ref, vp_ref, vc_ref, vn_ref, kx_ref, vx_ref,
               o_ref, kbuf, vbuf):
    i = pl.program_id(1)
    kbuf[0:WA_BLOCK] = kp_ref[0]
    kbuf[WA_BLOCK:WA_BLOCK + WA_QT] = kc_ref[0]
    kbuf[WA_BLOCK + WA_QT:] = kn_ref[0]
    vbuf[0:WA_BLOCK] = vp_ref[0]
    vbuf[WA_BLOCK:WA_BLOCK + WA_QT] = vc_ref[0]
    vbuf[WA_BLOCK + WA_QT:] = vn_ref[0]
    lo, m_lo, m_hi = _lane_masks()
    n_loc = 3 * WA_BLOCK
    q_off = lax.broadcasted_iota(jnp.int32, (WA_BLOCK, n_loc), 0)
    k_off = lax.broadcasted_iota(jnp.int32, (WA_BLOCK, n_loc), 1) - WA_BLOCK
    band = jnp.abs(k_off - q_off) <= WA_WINDOW
    n_pairs = WA_HEADS // 2

    def geometry(u):
        t, g = divmod(u, n_pairs)
        return t, g, slice(t * WA_BLOCK, (t + 1) * WA_BLOCK), slice(g * LANES, (g + 1) * LANES)

    def scores(u):
        t, _, rows, cols = geometry(u)
        qp = q_ref[0, rows, cols]
        q2 = jnp.concatenate([qp * m_lo, qp * m_hi], axis=0)
        s_loc = lax.dot_general(q2, kbuf[t * WA_BLOCK:t * WA_BLOCK + n_loc, :], _NT,
                                preferred_element_type=_F32)
        s_cx = lax.dot_general(q2, kx_ref[0], _NT, preferred_element_type=_F32)
        return s_loc, s_cx

    def finish(u, s):
        t, g, rows, cols = geometry(u)
        s_loc, s_cx = s
        k_pos = (i * WA_SUB + t) * WA_BLOCK + k_off
        valid = band & (k_pos >= 0) & (k_pos < SEQ)
        s_loc = jnp.where(valid[None], s_loc.reshape(2, WA_BLOCK, n_loc), NEG_INF)
        s_loc = s_loc.reshape(2 * WA_BLOCK, n_loc)
        sink_col = jnp.concatenate(
            [jnp.full((WA_BLOCK, 1), sink_ref[2 * g + hh], _F32) for hh in range(2)], axis=0)
        o2 = _softmax_pv([s_loc, s_cx],
                         [vbuf[t * WA_BLOCK:t * WA_BLOCK + n_loc, :], vx_ref[0]],
                         extra_logit=sink_col)
        o_ref[0, rows, cols] = jnp.where(lo, o2[:WA_BLOCK], o2[WA_BLOCK:]).astype(_BF16)

    _emit_pipelined(WA_SUB * n_pairs, WA_DEPTH, scores, finish)


def _wa_attention(qb, kb, vb, sink):
    n_blk = SEQ // WA_BLOCK
    qspec = pl.BlockSpec((1, WA_QT, WA_W), lambda b, i: (b, i, 0))
    cur = pl.BlockSpec((1, WA_QT, WA_KV_W), lambda b, i: (b, i, 0))
    prev = pl.BlockSpec((1, WA_BLOCK, WA_KV_W),
                        lambda b, i: (b, jnp.maximum(i * WA_SUB - 1, 0), 0))
    nxt = pl.BlockSpec((1, WA_BLOCK, WA_KV_W),
                       lambda b, i: (b, jnp.minimum((i + 1) * WA_SUB, n_blk - 1), 0))
    ctx = pl.BlockSpec((1, CTX_LEN, WA_KV_W), lambda b, i: (b, SEQ // CTX_LEN, 0))
    return pl.pallas_call(
        _wa_kernel,
        grid=(BATCH, SEQ // WA_QT),
        in_specs=[pl.BlockSpec(memory_space=pltpu.SMEM),
                  qspec, prev, cur, nxt, prev, cur, nxt, ctx, ctx],
        out_specs=qspec,
        out_shape=jax.ShapeDtypeStruct((BATCH, TOK, WA_W), _BF16),
        scratch_shapes=[pltpu.VMEM((WA_QT + 2 * WA_BLOCK, WA_KV_W), _BF16),
                        pltpu.VMEM((WA_QT + 2 * WA_BLOCK, WA_KV_W), _BF16)],
        compiler_params=pltpu.CompilerParams(
            dimension_semantics=("arbitrary", "arbitrary"), vmem_limit_bytes=VMEM_LIMIT),
        name="wa_attention",
    )(sink, qb, kb, kb, kb, vb, vb, vb, kb, vb)


def _ctx_kernel(sink_ref, qa_ref, ka_ref, va_ref, qb_ref, kb_ref, vb_ref, oa_in, ob_in,
                oa_ref, ob_ref):
    del oa_in, ob_in
    lo, m_lo, m_hi = _lane_masks()
    qa = qa_ref[0]
    outs = []
    for p in range(NA_HEADS // 2):
        cols = slice(p * LANES, (p + 1) * LANES)
        qp = qa[:, cols]
        q2 = jnp.concatenate([qp * m_lo, qp * m_hi], axis=0)
        s = lax.dot_general(q2, ka_ref[0, :, cols], _NT, preferred_element_type=_F32)
        o2 = _softmax_pv([s], [va_ref[0, :, cols]])
        outs.append(jnp.where(lo, o2[:CTX_LEN], o2[CTX_LEN:]))
    oa_ref[0] = jnp.concatenate(outs, axis=1).astype(_BF16)

    q8 = _stack_heads(qb_ref[0], m_lo, m_hi)
    s = lax.dot_general(q8, kb_ref[0], _NT, preferred_element_type=_F32)
    o8 = _softmax_pv([s], [vb_ref[0]], extra_logit=_sink_column(sink_ref, CTX_LEN))
    ob_ref[0] = _unstack_heads(o8, CTX_LEN, lo).astype(_BF16)


def _ctx_attention(qa, ka, va, qb, kb, vb, sink, attn_a, attn_b):
    ctx = lambda w: pl.BlockSpec((1, CTX_LEN, w), lambda b: (b, SEQ // CTX_LEN, 0))
    return pl.pallas_call(
        _ctx_kernel,
        grid=(BATCH,),
        in_specs=[pl.BlockSpec(memory_space=pltpu.SMEM),
                  ctx(NA_W), ctx(NA_W), ctx(NA_W), ctx(WA_W), ctx(WA_KV_W), ctx(WA_KV_W),
                  pl.BlockSpec(memory_space=pl.ANY), pl.BlockSpec(memory_space=pl.ANY)],
        out_specs=[ctx(NA_W), ctx(WA_W)],
        out_shape=[jax.ShapeDtypeStruct(attn_a.shape, _BF16),
                   jax.ShapeDtypeStruct(attn_b.shape, _BF16)],
        input_output_aliases={7: 0, 8: 1},
        compiler_params=pltpu.CompilerParams(
            dimension_semantics=("arbitrary",), vmem_limit_bytes=VMEM_LIMIT),
        name="ctx_attention",
    )(sink, qa, ka, va, qb, kb, vb, attn_a, attn_b)


def _outproj_kernel(x_ref, mod_ref, aa_ref, ab_ref, wo_ref, o_ref):
    y = (jnp.dot(aa_ref[0], wo_ref[:NA_W], preferred_element_type=_F32)
         + jnp.dot(ab_ref[0], wo_ref[NA_W:], preferred_element_type=_F32))
    o_ref[0] = x_ref[0] + mod_ref[0][5:6] * y


def _outproj(x_all, mod_l, attn_a, attn_b, w_out):
    tok = lambda w: pl.BlockSpec((1, TM, w), lambda b, t: (b, t, 0))
    return pl.pallas_call(
        _outproj_kernel,
        grid=(BATCH, N_TILES),
        in_specs=[
            tok(D_MODEL),
            pl.BlockSpec((1, N_MOD, D_MODEL), lambda b, t: (_cond_index(b, t), 0, 0)),
            tok(NA_W), tok(WA_W),
            pl.BlockSpec((Q_W, D_MODEL), lambda b, t: (0, 0)),
        ],
        out_specs=tok(D_MODEL),
        out_shape=jax.ShapeDtypeStruct((BATCH, TOK, D_MODEL), _F32),
        compiler_params=pltpu.CompilerParams(
            dimension_semantics=("arbitrary", "arbitrary"), vmem_limit_bytes=VMEM_LIMIT),
        name="out_proj",
    )(x_all, mod_l, attn_a, attn_b, w_out)


def _rope_tables():
    rot = HEAD_DIM // 2
    inv_freq = ROPE_BASE ** (-jnp.arange(0, rot, 2, dtype=_F32) / rot)
    t = jnp.arange(SEQ)
    row = (t // GRID_W).astype(_F32)
    col = (t % GRID_W).astype(_F32)
    ang = jnp.stack([row[:, None] * inv_freq, col[:, None] * inv_freq], axis=1)
    d = np.arange(LANES) % HEAD_DIM
    axis, half, freq = d // rot, (d % rot) // (rot // 2), d % (rot // 2)
    ang_l = ang[:, axis, freq]
    sign = jnp.asarray(np.where(half == 0, -1.0, 1.0), _F32)
    cos_t = jnp.concatenate([jnp.cos(ang_l), jnp.ones((CTX_LEN, LANES), _F32)], axis=0)
    sin_t = jnp.concatenate([jnp.sin(ang_l) * sign, jnp.zeros((CTX_LEN, LANES), _F32)], axis=0)
    return cos_t, sin_t


def _na_bias_table(rpb):
    col = np.arange(GRID_W)
    c0 = np.clip(col - NA_KW // 2, 0, GRID_W - NA_KW)
    in_win = (col[None, :] >= c0[:, None]) & (col[None, :] < c0[:, None] + NA_KW)
    dc = np.clip(col[None, :] - col[:, None] + (NA_KW - 1), 0, 2 * NA_KW - 2)
    full = jnp.where(in_win[None, None], rpb.astype(_F32)[:, :, dc], NEG_INF)
    return jnp.concatenate([full[:, :-1], full[:, 1:]], axis=-1)


def kernel(x, c, ctx, c_ctx, w_ada, b_ada, norm_g, w_ffn_up, w_ffn_down, w_in, w_out,
           qk_norm_g, na_rpb, wa_sink):
    q_cols = np.arange(IN_W)
    q_cols[NA_W:Q_W] = NA_W + (WA_PERM[:, None] * HEAD_DIM + np.arange(HEAD_DIM)[None]).reshape(-1)
    o_rows = q_cols[:Q_W]
    w_up_b = w_ffn_up.astype(_BF16)
    w_dn_b = w_ffn_down.astype(_BF16)
    w_in_b = w_in[:, :, q_cols].astype(_BF16)
    w_out_b = w_out[:, o_rows, :].astype(_BF16)
    sink_p = wa_sink[:, WA_PERM].astype(_F32)
    g4 = jnp.tile(qk_norm_g.astype(_F32), (1, 1, NA_W // HEAD_DIM))
    blk = np.arange(MXU_TILE) // HEAD_DIM
    bd = jnp.asarray((blk[:, None] == blk[None, :]) / HEAD_DIM, _BF16)
    cos_t, sin_t = _rope_tables()

    cond = jnp.zeros((N_COND, D_MODEL), _F32).at[:BATCH].set(c).at[CTX_COND].set(c_ctx)
    mods = _ada_params(cond, w_ada, b_ada).reshape(DEPTH, N_COND, N_MOD, D_MODEL)

    x_all = jnp.concatenate([x, ctx], axis=1)
    for l in range(DEPTH):
        last = l == DEPTH - 1
        mod_l = mods[l]
        x_all = _ffn(x_all, mod_l, norm_g[l, 0:1], w_up_b[l, 0], w_dn_b[l, 0], 0, N_TILES)
        qa, qb, ka, va, kb, vb = _proj(x_all, mod_l, norm_g[l, 1:2], w_in_b[l], g4[l], bd,
                                       cos_t, sin_t)
        attn_a = _na_attention(qa, ka, va, _na_bias_table(na_rpb[l]))
        attn_b = _wa_attention(qb, kb, vb, sink_p[l])
        attn_a, attn_b = _ctx_attention(qa, ka, va, qb, kb, vb, sink_p[l], attn_a, attn_b)
        x_all = _outproj(x_all, mod_l, attn_a, attn_b, w_out_b[l])
        x_all = _ffn(x_all, mod_l, norm_g[l, 2:3], w_up_b[l, 1], w_dn_b[l, 1], 6,
                     CTX_TILE if last else N_TILES)
    return x_all
```

```python
import functools

import numpy as np
import jax
import jax.numpy as jnp
from jax import lax
from jax.experimental import pallas as pl
from jax.experimental.pallas import tpu as pltpu

D_MODEL = 1024
BATCH = 4
SEQ = 8192
DEPTH = 4
CTX_LEN = 256
GRID_W = 64
HEAD_DIM = 64
N_HEADS = D_MODEL // HEAD_DIM
NA_HEADS = N_HEADS // 2
WA_HEADS = N_HEADS - NA_HEADS
WA_KV_HEADS = max(1, WA_HEADS // 4)
NA_KH = 8
NA_KW = 16
WA_WINDOW = 128
WA_BLOCK = 128
D_FF = ((8 * D_MODEL // 3 + 127) // 128) * 128
ROPE_BASE = 10000.0
N_MOD = 9
MACARON_W = 0.5
NORM_EPS = 1e-6
NEG_INF = -1e30
NA_W = NA_HEADS * HEAD_DIM
WA_W = WA_HEADS * HEAD_DIM
WA_KV_W = WA_KV_HEADS * HEAD_DIM
Q_W = NA_W + WA_W
IN_W = Q_W + 2 * NA_W + 2 * WA_KV_W

ROWS = SEQ // GRID_W
TOK = SEQ + CTX_LEN
TM = 256
N_TILES = TOK // TM
CTX_TILE = SEQ // TM
N_COND = 8
CTX_COND = BATCH
NA_RB = 8
NA_QT = NA_RB * GRID_W
WA_SUB = 4
WA_QT = WA_SUB * WA_BLOCK
STAGE_LAGS = (0, 1, 2, 3)
LANES = 128
MXU_TILE = 256
VMEM_LIMIT = 56 * 1024 * 1024

WA_PERM = np.array([0, 4, 1, 5, 2, 6, 3, 7])

_F32 = jnp.float32
_BF16 = jnp.bfloat16
_NT = (((1,), (1,)), ((), ()))


def _sigmoid(v):
    return 1.0 / (1.0 + jnp.exp(-v))


def _norm_mod(x, g, shift, scale):
    ms = jnp.mean(x * x, axis=-1, keepdims=True)
    h = x * lax.rsqrt(ms + NORM_EPS) * g
    return h * (1.0 + scale) + shift


def _cond_index(b, t):
    return jnp.where(t == CTX_TILE, CTX_COND, b)


ADA_TN = 1152


def _ada_kernel(cond_ref, w_ref, b_ref, o_ref):
    cond = cond_ref[...]
    a = (cond * _sigmoid(cond)).astype(_BF16)
    o_ref[0] = jnp.dot(a, w_ref[0].astype(_BF16), preferred_element_type=_F32) + b_ref[0]


def _ada_params(cond, w_ada, b_ada):
    n_out = N_MOD * D_MODEL
    return pl.pallas_call(
        _ada_kernel,
        grid=(DEPTH, n_out // ADA_TN),
        in_specs=[
            pl.BlockSpec((N_COND, D_MODEL), lambda l, n: (0, 0)),
            pl.BlockSpec((1, D_MODEL, ADA_TN), lambda l, n: (l, 0, n)),
            pl.BlockSpec((1, 1, ADA_TN), lambda l, n: (l, 0, n)),
        ],
        out_specs=pl.BlockSpec((1, N_COND, ADA_TN), lambda l, n: (l, 0, n)),
        out_shape=jax.ShapeDtypeStruct((DEPTH, N_COND, n_out), _F32),
        compiler_params=pltpu.CompilerParams(vmem_limit_bytes=VMEM_LIMIT),
        name="ada_params",
    )(cond, w_ada, b_ada.reshape(DEPTH, 1, n_out))


def _ffn_kernel(x_ref, mod_ref, g_ref, wup_ref, wdn_ref, o_ref, *, j):
    x = x_ref[0]
    mod = mod_ref[0]
    h = _norm_mod(x, g_ref[...], mod[j:j + 1], mod[j + 1:j + 2])
    u = jnp.dot(h.astype(_BF16), wup_ref[...], preferred_element_type=_F32)
    gt = u[:, :D_FF]
    up = u[:, D_FF:]
    a = (gt * _sigmoid(gt) * up).astype(_BF16)
    y = jnp.dot(a, wdn_ref[...], preferred_element_type=_F32)
    o_ref[0] = x + (MACARON_W * mod[j + 2:j + 3]) * y


def _ffn(x_all, mod_l, g, w_up, w_dn, j, n_tiles):
    const = pl.Buffered(1)
    return pl.pallas_call(
        functools.partial(_ffn_kernel, j=j),
        grid=(BATCH, n_tiles),
        in_specs=[
            pl.BlockSpec((1, TM, D_MODEL), lambda b, t: (b, t, 0)),
            pl.BlockSpec((1, N_MOD, D_MODEL), lambda b, t: (_cond_index(b, t), 0, 0)),
            pl.BlockSpec((1, D_MODEL), lambda b, t: (0, 0)),
            pl.BlockSpec((D_MODEL, 2 * D_FF), lambda b, t: (0, 0), pipeline_mode=const),
            pl.BlockSpec((D_FF, D_MODEL), lambda b, t: (0, 0), pipeline_mode=const),
        ],
        out_specs=pl.BlockSpec((1, TM, D_MODEL), lambda b, t: (b, t, 0)),
        out_shape=jax.ShapeDtypeStruct((BATCH, n_tiles * TM, D_MODEL), _F32),
        compiler_params=pltpu.CompilerParams(
            dimension_semantics=("arbitrary", "arbitrary"), vmem_limit_bytes=VMEM_LIMIT),
        name="ffn",
    )(x_all, mod_l, g, w_up, w_dn)


def _head_norm(t, g, bd):
    width = t.shape[1]
    outs = []
    for k in range(0, width, MXU_TILE):
        w = min(MXU_TILE, width - k)
        tc = t[:, k:k + w]
        sq = tc * tc
        hi = sq.astype(_BF16)
        lo = (sq - hi.astype(_F32)).astype(_BF16)
        bdk = bd[:w, :w]
        ms = (jnp.dot(hi, bdk, preferred_element_type=_F32)
              + jnp.dot(lo, bdk, preferred_element_type=_F32))
        outs.append(tc * lax.rsqrt(ms + NORM_EPS))
    out = outs[0] if len(outs) == 1 else jnp.concatenate(outs, axis=-1)
    return out * g


def _rope(t, cos, sin_signed, first_half):
    outs = []
    for k in range(0, t.shape[1], LANES):
        tc = t[:, k:k + LANES]
        partner = jnp.where(first_half,
                            pltpu.roll(tc, LANES - HEAD_DIM // 4, 1),
                            pltpu.roll(tc, HEAD_DIM // 4, 1))
        outs.append(tc * cos + partner * sin_signed)
    return outs[0] if len(outs) == 1 else jnp.concatenate(outs, axis=-1)


def _proj_kernel(x_ref, mod_ref, g_ref, win_ref, g4_ref, bd_ref, cos_ref, sin_ref,
                 qa_ref, qb_ref, ka_ref, va_ref, kb_ref, vb_ref):
    x = x_ref[0]
    mod = mod_ref[0]
    h = _norm_mod(x, g_ref[...], mod[3:4], mod[4:5])
    px = jnp.dot(h.astype(_BF16), win_ref[...], preferred_element_type=_F32)
    bd = bd_ref[...]
    g4 = g4_ref[...]
    cos = cos_ref[...]
    sin = sin_ref[...]
    lane = lax.broadcasted_iota(jnp.int32, (1, LANES), 1)
    first_half = (lane % (HEAD_DIM // 2)) < (HEAD_DIM // 4)
    q_scale = HEAD_DIM ** -0.5

    o1, o2, o3 = Q_W, Q_W + NA_W, Q_W + 2 * NA_W
    o4 = o3 + WA_KV_W
    qa = _head_norm(px[:, :NA_W], g4[0:1, :NA_W], bd)
    qa_ref[0] = (qa * q_scale).astype(_BF16)
    qb = _rope(_head_norm(px[:, NA_W:Q_W], g4[2:3, :WA_W], bd), cos, sin, first_half)
    qb_ref[0] = (qb * q_scale).astype(_BF16)
    ka = _head_norm(px[:, o1:o2], g4[1:2, :NA_W], bd)
    ka_ref[0] = ka.astype(_BF16)
    va_ref[0] = px[:, o2:o3].astype(_BF16)
    kb = _rope(_head_norm(px[:, o3:o4], g4[3:4, :WA_KV_W], bd), cos, sin, first_half)
    kb_ref[0] = kb.astype(_BF16)
    vb_ref[0] = px[:, o4:].astype(_BF16)


def _proj(x_all, mod_l, g, w_in, g4, bd, cos_t, sin_t):
    const = pl.Buffered(1)
    tok = lambda w: pl.BlockSpec((1, TM, w), lambda b, t: (b, t, 0))
    shp = lambda w: jax.ShapeDtypeStruct((BATCH, TOK, w), _BF16)
    return pl.pallas_call(
        _proj_kernel,
        grid=(BATCH, N_TILES),
        in_specs=[
            tok(D_MODEL),
            pl.BlockSpec((1, N_MOD, D_MODEL), lambda b, t: (_cond_index(b, t), 0, 0)),
            pl.BlockSpec((1, D_MODEL), lambda b, t: (0, 0)),
            pl.BlockSpec((D_MODEL, IN_W), lambda b, t: (0, 0), pipeline_mode=const),
            pl.BlockSpec((4, NA_W), lambda b, t: (0, 0)),
            pl.BlockSpec((MXU_TILE, MXU_TILE), lambda b, t: (0, 0)),
            pl.BlockSpec((TM, LANES), lambda b, t: (t, 0)),
            pl.BlockSpec((TM, LANES), lambda b, t: (t, 0)),
        ],
        out_specs=[tok(NA_W), tok(WA_W), tok(NA_W), tok(NA_W), tok(WA_KV_W), tok(WA_KV_W)],
        out_shape=[shp(NA_W), shp(WA_W), shp(NA_W), shp(NA_W), shp(WA_KV_W), shp(WA_KV_W)],
        compiler_params=pltpu.CompilerParams(
            dimension_semantics=("arbitrary", "arbitrary"), vmem_limit_bytes=VMEM_LIMIT),
        name="qkv_proj",
    )(x_all, mod_l, g, w_in, g4, bd, cos_t, sin_t)


def _lane_masks():
    lane = lax.broadcasted_iota(jnp.int32, (1, LANES), 1)
    lo = lane < HEAD_DIM
    return lo, lo.astype(_BF16), (~lo).astype(_BF16)


def _softmax_pv(scores, values, extra_logit=None):
    m = _row_max(scores, extra_logit)
    return _normalise(_exp_pv(scores, m, values, extra_logit))


def _row_max(scores, extra_logit=None):
    m = jnp.max(_fold_lane_groups(scores, jnp.maximum), axis=-1, keepdims=True)
    return m if extra_logit is None else jnp.maximum(m, extra_logit)


def _exp_pv(scores, m, values, extra_logit=None):
    acc = None
    for s, v in zip(scores, values):
        v1 = jnp.concatenate([v, jnp.ones((v.shape[0], LANES), _BF16)], axis=1)
        pv = jnp.dot(jnp.exp(s - m).astype(_BF16), v1, preferred_element_type=_F32)
        acc = pv if acc is None else acc + pv
    if extra_logit is not None:
        acc = jnp.concatenate([acc[:, :LANES], acc[:, LANES:] + jnp.exp(extra_logit - m)], axis=1)
    return acc


def _normalise(acc):
    return acc[:, :LANES] / acc[:, LANES:]


def _fold_lane_groups(blocks, op):
    acc = None
    for s in blocks:
        for k in range(0, s.shape[1], LANES):
            c = s[:, k:k + LANES]
            acc = c if acc is None else op(acc, c)
    return acc


def _emit_staged(n_units, stages, lags):
    state = {}
    for step in range(n_units + lags[-1]):
        for k, (stage, lag) in enumerate(zip(stages, lags)):
            u = step - lag
            if 0 <= u < n_units:
                state[u] = stage(u, state.get(u))


def _na_kernel(q_ref, kp_ref, kc_ref, kn_ref, vp_ref, vc_ref, vn_ref, kx_ref, vx_ref,
               tbl_ref, o_ref, kbuf, vbuf):
    rb = pl.program_id(1)
    kbuf[0:NA_QT] = kp_ref[0]
    kbuf[NA_QT:2 * NA_QT] = kc_ref[0]
    kbuf[2 * NA_QT:3 * NA_QT] = kn_ref[0]
    vbuf[0:NA_QT] = vp_ref[0]
    vbuf[NA_QT:2 * NA_QT] = vc_ref[0]
    vbuf[2 * NA_QT:3 * NA_QT] = vn_ref[0]
    lo, m_lo, m_hi = _lane_masks()
    n_win = NA_KH * GRID_W

    n_pairs = NA_HEADS // 2

    def geometry(u):
        j, p = divmod(u, n_pairs)
        r = rb * NA_RB + j
        r0 = jnp.clip(r - NA_KH // 2, 0, ROWS - NA_KH)
        dr0 = r0 - r + (NA_KH - 1)
        off = pl.multiple_of((r0 - rb * NA_RB + NA_RB) * GRID_W, GRID_W)
        return j, p, dr0, off, slice(p * LANES, (p + 1) * LANES)

    def score_stage(u, _):
        j, _, _, off, cols = geometry(u)
        qp = q_ref[0, j * GRID_W:(j + 1) * GRID_W, cols]
        q2 = jnp.concatenate([qp * m_lo, qp * m_hi], axis=0)
        s_nb = lax.dot_general(q2, kbuf[pl.ds(off, n_win), cols], _NT,
                               preferred_element_type=_F32)
        s_cx = lax.dot_general(q2, kx_ref[0, :, cols], _NT, preferred_element_type=_F32)
        return s_nb, s_cx

    def max_stage(u, s):
        _, p, dr0, _, _ = geometry(u)
        s_nb, s_cx = s
        bias = jnp.concatenate([
            jnp.concatenate([tbl_ref[2 * p + hh, pl.ds(dr0 + 2 * k, 1)][0]
                             for k in range(NA_KH // 2)], axis=1)
            for hh in range(2)], axis=0)
        sc = [s_nb + bias, s_cx]
        return sc, _row_max(sc)

    def pv_stage(u, s):
        _, _, _, off, cols = geometry(u)
        sc, m = s
        return _exp_pv(sc, m, [vbuf[pl.ds(off, n_win), cols], vx_ref[0, :, cols]])

    def out_stage(u, s):
        j, _, _, _, cols = geometry(u)
        o2 = _normalise(s)
        o_ref[0, j * GRID_W:(j + 1) * GRID_W, cols] = (
            jnp.where(lo, o2[:GRID_W], o2[GRID_W:]).astype(_BF16))

    _emit_staged(NA_RB * n_pairs, [score_stage, max_stage, pv_stage, out_stage], STAGE_LAGS)


def _na_attention(qa, ka, va, tbl):
    n_rb = ROWS // NA_RB
    blk = lambda f: pl.BlockSpec((1, NA_QT, NA_W), f)
    prev = lambda b, i: (b, jnp.maximum(i - 1, 0), 0)
    cur = lambda b, i: (b, i, 0)
    nxt = lambda b, i: (b, jnp.minimum(i + 1, n_rb - 1), 0)
    ctx = pl.BlockSpec((1, CTX_LEN, NA_W), lambda b, i: (b, SEQ // CTX_LEN, 0))
    return pl.pallas_call(
        _na_kernel,
        grid=(BATCH, n_rb),
        in_specs=[blk(cur), blk(prev), blk(cur), blk(nxt), blk(prev), blk(cur), blk(nxt),
                  ctx, ctx,
                  pl.BlockSpec(tbl.shape, lambda b, i: (0, 0, 0, 0))],
        out_specs=blk(cur),
        out_shape=jax.ShapeDtypeStruct((BATCH, TOK, NA_W), _BF16),
        scratch_shapes=[pltpu.VMEM((3 * NA_QT, NA_W), _BF16),
                        pltpu.VMEM((3 * NA_QT, NA_W), _BF16)],
        compiler_params=pltpu.CompilerParams(
            dimension_semantics=("arbitrary", "arbitrary"), vmem_limit_bytes=VMEM_LIMIT),
        name="na_attention",
    )(qa, ka, ka, ka, va, va, va, ka, va, tbl)


def _stack_heads(q, m_lo, m_hi):
    parts = []
    for pos in range(WA_HEADS):
        qp = q[:, (pos // 2) * LANES:(pos // 2 + 1) * LANES]
        parts.append(qp * (m_lo if pos % 2 == 0 else m_hi))
    return jnp.concatenate(parts, axis=0)


def _unstack_heads(o, rows, lo):
    outs = []
    for g in range(WA_HEADS // 2):
        outs.append(jnp.where(lo, o[(2 * g) * rows:(2 * g + 1) * rows],
                              o[(2 * g + 1) * rows:(2 * g + 2) * rows]))
    return jnp.concatenate(outs, axis=1)


def _sink_column(sink_ref, rows):
    return jnp.concatenate(
        [jnp.full((rows, 1), sink_ref[pos], _F32) for pos in range(WA_HEADS)], axis=0)


def _wa_kernel(sink_ref, q_ref, kp_ref, kc_ref, kn_ref, vp_ref, vc_ref, vn_ref, kx_ref, vx_ref,
               o_ref, kbuf, vbuf):
    i = pl.program_id(1)
    kbuf[0:WA_BLOCK] = kp_ref[0]
    kbuf[WA_BLOCK:WA_BLOCK + WA_QT] = kc_ref[0]
    kbuf[WA_BLOCK + WA_QT:] = kn_ref[0]
    vbuf[0:WA_BLOCK] = vp_ref[0]
    vbuf[WA_BLOCK:WA_BLOCK + WA_QT] = vc_ref[0]
    vbuf[WA_BLOCK + WA_QT:] = vn_ref[0]
    lo, m_lo, m_hi = _lane_masks()
    n_loc = 3 * WA_BLOCK
    q_off = lax.broadcasted_iota(jnp.int32, (WA_BLOCK, n_loc), 0)
    k_off = lax.broadcasted_iota(jnp.int32, (WA_BLOCK, n_loc), 1) - WA_BLOCK
    band = jnp.abs(k_off - q_off) <= WA_WINDOW
    n_pairs = WA_HEADS // 2

    def geometry(u):
        t, g = divmod(u, n_pairs)
        return t, g, slice(t * WA_BLOCK, (t + 1) * WA_BLOCK), slice(g * LANES, (g + 1) * LANES)

    def sink_col(g):
        return jnp.concatenate(
            [jnp.full((WA_BLOCK, 1), sink_ref[2 * g + hh], _F32) for hh in range(2)], axis=0)

    def score_stage(u, _):
        t, _, rows, cols = geometry(u)
        qp = q_ref[0, rows, cols]
        q2 = jnp.concatenate([qp * m_lo, qp * m_hi], axis=0)
        s_loc = lax.dot_general(q2, kbuf[t * WA_BLOCK:t * WA_BLOCK + n_loc, :], _NT,
                                preferred_element_type=_F32)
        s_cx = lax.dot_general(q2, kx_ref[0], _NT, preferred_element_type=_F32)
        return s_loc, s_cx

    def max_stage(u, s):
        t, g, _, _ = geometry(u)
        s_loc, s_cx = s
        k_pos = (i * WA_SUB + t) * WA_BLOCK + k_off
        valid = band & (k_pos >= 0) & (k_pos < SEQ)
        s_loc = jnp.where(valid[None], s_loc.reshape(2, WA_BLOCK, n_loc), NEG_INF)
        sc = [s_loc.reshape(2 * WA_BLOCK, n_loc), s_cx]
        return sc, _row_max(sc, sink_col(g))

    def pv_stage(u, s):
        t, g, _, _ = geometry(u)
        sc, m = s
        return _exp_pv(sc, m, [vbuf[t * WA_BLOCK:t * WA_BLOCK + n_loc, :], vx_ref[0]],
                       sink_col(g))

    def out_stage(u, s):
        _, _, rows, cols = geometry(u)
        o2 = _normalise(s)
        o_ref[0, rows, cols] = jnp.where(lo, o2[:WA_BLOCK], o2[WA_BLOCK:]).astype(_BF16)

    _emit_staged(WA_SUB * n_pairs, [score_stage, max_stage, pv_stage, out_stage], STAGE_LAGS)


def _wa_attention(qb, kb, vb, sink):
    n_blk = SEQ // WA_BLOCK
    qspec = pl.BlockSpec((1, WA_QT, WA_W), lambda b, i: (b, i, 0))
    cur = pl.BlockSpec((1, WA_QT, WA_KV_W), lambda b, i: (b, i, 0))
    prev = pl.BlockSpec((1, WA_BLOCK, WA_KV_W),
                        lambda b, i: (b, jnp.maximum(i * WA_SUB - 1, 0), 0))
    nxt = pl.BlockSpec((1, WA_BLOCK, WA_KV_W),
                       lambda b, i: (b, jnp.minimum((i + 1) * WA_SUB, n_blk - 1), 0))
    ctx = pl.BlockSpec((1, CTX_LEN, WA_KV_W), lambda b, i: (b, SEQ // CTX_LEN, 0))
    return pl.pallas_call(
        _wa_kernel,
        grid=(BATCH, SEQ // WA_QT),
        in_specs=[pl.BlockSpec(memory_space=pltpu.SMEM),
                  qspec, prev, cur, nxt, prev, cur, nxt, ctx, ctx],
        out_specs=qspec,
        out_shape=jax.ShapeDtypeStruct((BATCH, TOK, WA_W), _BF16),
        scratch_shapes=[pltpu.VMEM((WA_QT + 2 * WA_BLOCK, WA_KV_W), _BF16),
                        pltpu.VMEM((WA_QT + 2 * WA_BLOCK, WA_KV_W), _BF16)],
        compiler_params=pltpu.CompilerParams(
            dimension_semantics=("arbitrary", "arbitrary"), vmem_limit_bytes=VMEM_LIMIT),
        name="wa_attention",
    )(sink, qb, kb, kb, kb, vb, vb, vb, kb, vb)


def _ctx_kernel(sink_ref, qa_ref, ka_ref, va_ref, qb_ref, kb_ref, vb_ref, oa_in, ob_in,
                oa_ref, ob_ref):
    del oa_in, ob_in
    lo, m_lo, m_hi = _lane_masks()
    qa = qa_ref[0]
    outs = []
    for p in range(NA_HEADS // 2):
        cols = slice(p * LANES, (p + 1) * LANES)
        qp = qa[:, cols]
        q2 = jnp.concatenate([qp * m_lo, qp * m_hi], axis=0)
        s = lax.dot_general(q2, ka_ref[0, :, cols], _NT, preferred_element_type=_F32)
        o2 = _softmax_pv([s], [va_ref[0, :, cols]])
        outs.append(jnp.where(lo, o2[:CTX_LEN], o2[CTX_LEN:]))
    oa_ref[0] = jnp.concatenate(outs, axis=1).astype(_BF16)

    q8 = _stack_heads(qb_ref[0], m_lo, m_hi)
    s = lax.dot_general(q8, kb_ref[0], _NT, preferred_element_type=_F32)
    o8 = _softmax_pv([s], [vb_ref[0]], extra_logit=_sink_column(sink_ref, CTX_LEN))
    ob_ref[0] = _unstack_heads(o8, CTX_LEN, lo).astype(_BF16)


def _ctx_attention(qa, ka, va, qb, kb, vb, sink, attn_a, attn_b):
    ctx = lambda w: pl.BlockSpec((1, CTX_LEN, w), lambda b: (b, SEQ // CTX_LEN, 0))
    return pl.pallas_call(
        _ctx_kernel,
        grid=(BATCH,),
        in_specs=[pl.BlockSpec(memory_space=pltpu.SMEM),
                  ctx(NA_W), ctx(NA_W), ctx(NA_W), ctx(WA_W), ctx(WA_KV_W), ctx(WA_KV_W),
                  pl.BlockSpec(memory_space=pl.ANY), pl.BlockSpec(memory_space=pl.ANY)],
        out_specs=[ctx(NA_W), ctx(WA_W)],
        out_shape=[jax.ShapeDtypeStruct(attn_a.shape, _BF16),
                   jax.ShapeDtypeStruct(attn_b.shape, _BF16)],
        input_output_aliases={7: 0, 8: 1},
        compiler_params=pltpu.CompilerParams(
            dimension_semantics=("arbitrary",), vmem_limit_bytes=VMEM_LIMIT),
        name="ctx_attention",
    )(sink, qa, ka, va, qb, kb, vb, attn_a, attn_b)


def _outproj_kernel(x_ref, mod_ref, aa_ref, ab_ref, wo_ref, o_ref):
    y = (jnp.dot(aa_ref[0], wo_ref[:NA_W], preferred_element_type=_F32)
         + jnp.dot(ab_ref[0], wo_ref[NA_W:], preferred_element_type=_F32))
    o_ref[0] = x_ref[0] + mod_ref[0][5:6] * y


def _outproj(x_all, mod_l, attn_a, attn_b, w_out):
    tok = lambda w: pl.BlockSpec((1, TM, w), lambda b, t: (b, t, 0))
    return pl.pallas_call(
        _outproj_kernel,
        grid=(BATCH, N_TILES),
        in_specs=[
            tok(D_MODEL),
            pl.BlockSpec((1, N_MOD, D_MODEL), lambda b, t: (_cond_index(b, t), 0, 0)),
            tok(NA_W), tok(WA_W),
            pl.BlockSpec((Q_W, D_MODEL), lambda b, t: (0, 0)),
        ],
        out_specs=tok(D_MODEL),
        out_shape=jax.ShapeDtypeStruct((BATCH, TOK, D_MODEL), _F32),
        compiler_params=pltpu.CompilerParams(
            dimension_semantics=("arbitrary", "arbitrary"), vmem_limit_bytes=VMEM_LIMIT),
        name="out_proj",
    )(x_all, mod_l, attn_a, attn_b, w_out)


def _rope_tables():
    rot = HEAD_DIM // 2
    inv_freq = ROPE_BASE ** (-jnp.arange(0, rot, 2, dtype=_F32) / rot)
    t = jnp.arange(SEQ)
    row = (t // GRID_W).astype(_F32)
    col = (t % GRID_W).astype(_F32)
    ang = jnp.stack([row[:, None] * inv_freq, col[:, None] * inv_freq], axis=1)
    d = np.arange(LANES) % HEAD_DIM
    axis, half, freq = d // rot, (d % rot) // (rot // 2), d % (rot // 2)
    ang_l = ang[:, axis, freq]
    sign = jnp.asarray(np.where(half == 0, -1.0, 1.0), _F32)
    cos_t = jnp.concatenate([jnp.cos(ang_l), jnp.ones((CTX_LEN, LANES), _F32)], axis=0)
    sin_t = jnp.concatenate([jnp.sin(ang_l) * sign, jnp.zeros((CTX_LEN, LANES), _F32)], axis=0)
    return cos_t, sin_t


def _na_bias_table(rpb):
    col = np.arange(GRID_W)
    c0 = np.clip(col - NA_KW // 2, 0, GRID_W - NA_KW)
    in_win = (col[None, :] >= c0[:, None]) & (col[None, :] < c0[:, None] + NA_KW)
    dc = np.clip(col[None, :] - col[:, None] + (NA_KW - 1), 0, 2 * NA_KW - 2)
    full = jnp.where(in_win[None, None], rpb.astype(_F32)[:, :, dc], NEG_INF)
    return jnp.concatenate([full[:, :-1], full[:, 1:]], axis=-1)


def kernel(x, c, ctx, c_ctx, w_ada, b_ada, norm_g, w_ffn_up, w_ffn_down, w_in, w_out,
           qk_norm_g, na_rpb, wa_sink):
    q_cols = np.arange(IN_W)
    q_cols[NA_W:Q_W] = NA_W + (WA_PERM[:, None] * HEAD_DIM + np.arange(HEAD_DIM)[None]).reshape(-1)
    o_rows = q_cols[:Q_W]
    w_up_b = w_ffn_up.astype(_BF16)
    w_dn_b = w_ffn_down.astype(_BF16)
    w_in_b = w_in[:, :, q_cols].astype(_BF16)
    w_out_b = w_out[:, o_rows, :].astype(_BF16)
    sink_p = wa_sink[:, WA_PERM].astype(_F32)
    g4 = jnp.tile(qk_norm_g.astype(_F32), (1, 1, NA_W // HEAD_DIM))
    blk = np.arange(MXU_TILE) // HEAD_DIM
    bd = jnp.asarray((blk[:, None] == blk[None, :]) / HEAD_DIM, _BF16)
    cos_t, sin_t = _rope_tables()

    cond = jnp.zeros((N_COND, D_MODEL), _F32).at[:BATCH].set(c).at[CTX_COND].set(c_ctx)
    mods = _ada_params(cond, w_ada, b_ada).reshape(DEPTH, N_COND, N_MOD, D_MODEL)

    x_all = jnp.concatenate([x, ctx], axis=1)
    for l in range(DEPTH):
        last = l == DEPTH - 1
        mod_l = mods[l]
        x_all = _ffn(x_all, mod_l, norm_g[l, 0:1], w_up_b[l, 0], w_dn_b[l, 0], 0, N_TILES)
        qa, qb, ka, va, kb, vb = _proj(x_all, mod_l, norm_g[l, 1:2], w_in_b[l], g4[l], bd,
                                       cos_t, sin_t)
        attn_a = _na_attention(qa, ka, va, _na_bias_table(na_rpb[l]))
        attn_b = _wa_attention(qb, kb, vb, sink_p[l])
        attn_a, attn_b = _ctx_attention(qa, ka, va, qb, kb, vb, sink_p[l], attn_a, attn_b)
        x_all = _outproj(x_all, mod_l, attn_a, attn_b, w_out_b[l])
        x_all = _ffn(x_all, mod_l, norm_g[l, 2:3], w_up_b[l, 1], w_dn_b[l, 1], 6,
                     CTX_TILE if last else N_TILES)
    return x_all
```

```python
import functools

import numpy as np
import jax
import jax.numpy as jnp
from jax import lax
from jax.experimental import pallas as pl
from jax.experimental.pallas import tpu as pltpu

D_MODEL = 1024
BATCH = 4
SEQ = 8192
DEPTH = 4
CTX_LEN = 256
GRID_W = 64
HEAD_DIM = 64
N_HEADS = D_MODEL // HEAD_DIM
NA_HEADS = N_HEADS // 2
WA_HEADS = N_HEADS - NA_HEADS
WA_KV_HEADS = max(1, WA_HEADS // 4)
NA_KH = 8
NA_KW = 16
WA_WINDOW = 128
WA_BLOCK = 128
D_FF = ((8 * D_MODEL // 3 + 127) // 128) * 128
ROPE_BASE = 10000.0
N_MOD = 9
MACARON_W = 0.5
NORM_EPS = 1e-6
NEG_INF = -1e30
NA_W = NA_HEADS * HEAD_DIM
WA_W = WA_HEADS * HEAD_DIM
WA_KV_W = WA_KV_HEADS * HEAD_DIM
Q_W = NA_W + WA_W
IN_W = Q_W + 2 * NA_W + 2 * WA_KV_W

ROWS = SEQ // GRID_W
TOK = SEQ + CTX_LEN
TM = 256
N_TILES = TOK // TM
CTX_TILE = SEQ // TM
N_COND = 8
CTX_COND = BATCH
NA_RB = 8
NA_QT = NA_RB * GRID_W
WA_SUB = 4
WA_QT = WA_SUB * WA_BLOCK
STAGE_LAGS = (0, 1, 2, 3)
LANES = 128
MXU_TILE = 256
VMEM_LIMIT = 56 * 1024 * 1024

_F32 = jnp.float32
_BF16 = jnp.bfloat16
_NT = (((1,), (1,)), ((), ()))


def _sigmoid(v):
    return 1.0 / (1.0 + jnp.exp(-v))


def _norm_mod(x, g, shift, scale):
    ms = jnp.mean(x * x, axis=-1, keepdims=True)
    h = x * lax.rsqrt(ms + NORM_EPS) * g
    return h * (1.0 + scale) + shift


def _cond_index(b, t):
    return jnp.where(t == CTX_TILE, CTX_COND, b)


ADA_TN = 1152


def _ada_kernel(cond_ref, w_ref, b_ref, o_ref):
    cond = cond_ref[...]
    a = (cond * _sigmoid(cond)).astype(_BF16)
    o_ref[0] = jnp.dot(a, w_ref[0].astype(_BF16), preferred_element_type=_F32) + b_ref[0]


def _ada_params(cond, w_ada, b_ada):
    n_out = N_MOD * D_MODEL
    return pl.pallas_call(
        _ada_kernel,
        grid=(DEPTH, n_out // ADA_TN),
        in_specs=[
            pl.BlockSpec((N_COND, D_MODEL), lambda l, n: (0, 0)),
            pl.BlockSpec((1, D_MODEL, ADA_TN), lambda l, n: (l, 0, n)),
            pl.BlockSpec((1, 1, ADA_TN), lambda l, n: (l, 0, n)),
        ],
        out_specs=pl.BlockSpec((1, N_COND, ADA_TN), lambda l, n: (l, 0, n)),
        out_shape=jax.ShapeDtypeStruct((DEPTH, N_COND, n_out), _F32),
        compiler_params=pltpu.CompilerParams(vmem_limit_bytes=VMEM_LIMIT),
        name="ada_params",
    )(cond, w_ada, b_ada.reshape(DEPTH, 1, n_out))


def _swiglu_half_step(x, mod, g, wup_ref, wdn_ref, j):
    h = _norm_mod(x, g, mod[j:j + 1], mod[j + 1:j + 2])
    u = jnp.dot(h.astype(_BF16), wup_ref[...], preferred_element_type=_F32)
    gt = u[:, :D_FF]
    up = u[:, D_FF:]
    a = (gt * _sigmoid(gt) * up).astype(_BF16)
    y = jnp.dot(a, wdn_ref[...], preferred_element_type=_F32)
    return x + (MACARON_W * mod[j + 2:j + 3]) * y


def _ffn_kernel(x_ref, mod_ref, g_ref, wup_ref, wdn_ref, o_ref):
    o_ref[0] = _swiglu_half_step(x_ref[0], mod_ref[0], g_ref[...], wup_ref, wdn_ref, 0)


def _entry_ffn_kernel(xl_ref, xc_ref, mod_ref, g_ref, wup_ref, wdn_ref, o_ref):
    x = jnp.where(pl.program_id(1) == CTX_TILE, xc_ref[0], xl_ref[0])
    o_ref[0] = _swiglu_half_step(x, mod_ref[0], g_ref[...], wup_ref, wdn_ref, 0)


def _mix_ffn_kernel(x_ref, mod_ref, g_ref, wup_ref, wdn_ref, al_ref, bl_ref, ac_ref, bc_ref,
                    wo_ref, o_ref):
    is_ctx = pl.program_id(1) == CTX_TILE
    aa = jnp.where(is_ctx, ac_ref[0], al_ref[0])
    ab = jnp.where(is_ctx, bc_ref[0], bl_ref[0])
    y = (jnp.dot(aa, wo_ref[:NA_W], preferred_element_type=_F32)
         + jnp.dot(ab, wo_ref[NA_W:], preferred_element_type=_F32))
    mod = mod_ref[0]
    x = x_ref[0] + mod[5:6] * y
    o_ref[0] = _swiglu_half_step(x, mod, g_ref[...], wup_ref, wdn_ref, 6)


def _ffn(x_in, mod_l, g, w_up, w_dn, n_tiles, mix=None):
    const = pl.Buffered(1)
    tok = lambda w: pl.BlockSpec((1, TM, w), lambda b, t: (b, t, 0))
    lat = lambda w: pl.BlockSpec((1, TM, w), lambda b, t: (b, jnp.minimum(t, CTX_TILE - 1), 0))
    ctx = lambda w: pl.BlockSpec((1, CTX_LEN, w), lambda b, t: (b, 0, 0))
    entry = isinstance(x_in, tuple)
    assert not (entry and mix is not None)
    in_specs = ([lat(D_MODEL), ctx(D_MODEL)] if entry else [tok(D_MODEL)]) + [
        pl.BlockSpec((1, N_MOD, D_MODEL), lambda b, t: (_cond_index(b, t), 0, 0)),
        pl.BlockSpec((1, D_MODEL), lambda b, t: (0, 0)),
        pl.BlockSpec((D_MODEL, 2 * D_FF), lambda b, t: (0, 0), pipeline_mode=const),
        pl.BlockSpec((D_FF, D_MODEL), lambda b, t: (0, 0), pipeline_mode=const),
    ]
    args = (list(x_in) if entry else [x_in]) + [mod_l, g, w_up, w_dn]
    kern, name = (_entry_ffn_kernel, "entry_ffn") if entry else (_ffn_kernel, "ffn")
    if mix is not None:
        kern, name = _mix_ffn_kernel, "mix_ffn"
        in_specs += [lat(NA_W), lat(WA_W), ctx(NA_W), ctx(WA_W),
                     pl.BlockSpec((Q_W, D_MODEL), lambda b, t: (0, 0), pipeline_mode=const)]
        args += list(mix)
    return pl.pallas_call(
        kern,
        grid=(BATCH, n_tiles),
        in_specs=in_specs,
        out_specs=tok(D_MODEL),
        out_shape=jax.ShapeDtypeStruct((BATCH, n_tiles * TM, D_MODEL), _F32),
        compiler_params=pltpu.CompilerParams(
            dimension_semantics=("arbitrary", "arbitrary"), vmem_limit_bytes=VMEM_LIMIT),
        name=name,
    )(*args)


def _head_norm(t, g, bd):
    width = t.shape[1]
    outs = []
    for k in range(0, width, MXU_TILE):
        w = min(MXU_TILE, width - k)
        tc = t[:, k:k + w]
        sq = tc * tc
        hi = sq.astype(_BF16)
        lo = (sq - hi.astype(_F32)).astype(_BF16)
        bdk = bd[:w, :w]
        ms = (jnp.dot(hi, bdk, preferred_element_type=_F32)
              + jnp.dot(lo, bdk, preferred_element_type=_F32))
        outs.append(tc * lax.rsqrt(ms + NORM_EPS))
    out = outs[0] if len(outs) == 1 else jnp.concatenate(outs, axis=-1)
    return out * g


def _rope(t, cos, sin_signed, first_half):
    outs = []
    for k in range(0, t.shape[1], LANES):
        tc = t[:, k:k + LANES]
        partner = jnp.where(first_half,
                            pltpu.roll(tc, LANES - HEAD_DIM // 4, 1),
                            pltpu.roll(tc, HEAD_DIM // 4, 1))
        outs.append(tc * cos + partner * sin_signed)
    return outs[0] if len(outs) == 1 else jnp.concatenate(outs, axis=-1)


def _proj_kernel(x_ref, mod_ref, g_ref, win_ref, g4_ref, bd_ref, cos_ref, sin_ref,
                 qa_ref, qb_ref, ka_ref, va_ref, kb_ref, vb_ref):
    x = x_ref[0]
    mod = mod_ref[0]
    h = _norm_mod(x, g_ref[...], mod[3:4], mod[4:5])
    px = jnp.dot(h.astype(_BF16), win_ref[...], preferred_element_type=_F32)
    bd = bd_ref[...]
    g4 = g4_ref[...]
    cos = cos_ref[...]
    sin = sin_ref[...]
    lane = lax.broadcasted_iota(jnp.int32, (1, LANES), 1)
    first_half = (lane % (HEAD_DIM // 2)) < (HEAD_DIM // 4)
    q_scale = HEAD_DIM ** -0.5

    o1, o2, o3 = Q_W, Q_W + NA_W, Q_W + 2 * NA_W
    o4 = o3 + WA_KV_W
    qa = _head_norm(px[:, :NA_W], g4[0:1, :NA_W], bd)
    qa_ref[0] = (qa * q_scale).astype(_BF16)
    qb = _rope(_head_norm(px[:, NA_W:Q_W], g4[2:3, :WA_W], bd), cos, sin, first_half)
    qb_ref[0] = (qb * q_scale).astype(_BF16)
    ka = _head_norm(px[:, o1:o2], g4[1:2, :NA_W], bd)
    ka_ref[0] = ka.astype(_BF16)
    va_ref[0] = px[:, o2:o3].astype(_BF16)
    kb = _rope(_head_norm(px[:, o3:o4], g4[3:4, :WA_KV_W], bd), cos, sin, first_half)
    kb_ref[0] = kb.astype(_BF16)
    vb_ref[0] = px[:, o4:].astype(_BF16)


def _proj(x_all, mod_l, g, w_in, g4, bd, cos_t, sin_t):
    const = pl.Buffered(1)
    tok = lambda w: pl.BlockSpec((1, TM, w), lambda b, t: (b, t, 0))
    shp = lambda w: jax.ShapeDtypeStruct((BATCH, TOK, w), _BF16)
    return pl.pallas_call(
        _proj_kernel,
        grid=(BATCH, N_TILES),
        in_specs=[
            tok(D_MODEL),
            pl.BlockSpec((1, N_MOD, D_MODEL), lambda b, t: (_cond_index(b, t), 0, 0)),
            pl.BlockSpec((1, D_MODEL), lambda b, t: (0, 0)),
            pl.BlockSpec((D_MODEL, IN_W), lambda b, t: (0, 0), pipeline_mode=const),
            pl.BlockSpec((4, NA_W), lambda b, t: (0, 0)),
            pl.BlockSpec((MXU_TILE, MXU_TILE), lambda b, t: (0, 0)),
            pl.BlockSpec((TM, LANES), lambda b, t: (t, 0)),
            pl.BlockSpec((TM, LANES), lambda b, t: (t, 0)),
        ],
        out_specs=[tok(NA_W), tok(WA_W), tok(NA_W), tok(NA_W), tok(WA_KV_W), tok(WA_KV_W)],
        out_shape=[shp(NA_W), shp(WA_W), shp(NA_W), shp(NA_W), shp(WA_KV_W), shp(WA_KV_W)],
        compiler_params=pltpu.CompilerParams(
            dimension_semantics=("arbitrary", "arbitrary"), vmem_limit_bytes=VMEM_LIMIT),
        name="qkv_proj",
    )(x_all, mod_l, g, w_in, g4, bd, cos_t, sin_t)


def _lane_masks():
    lane = lax.broadcasted_iota(jnp.int32, (1, LANES), 1)
    lo = lane < HEAD_DIM
    return lo, lo.astype(_BF16), (~lo).astype(_BF16)


def _softmax_pv(scores, values, extra_logit=None):
    m = _row_max(scores, extra_logit)
    return _normalise(_exp_pv(scores, m, values, extra_logit))


def _row_max(scores, extra_logit=None):
    m = jnp.max(_fold_lane_groups(scores, jnp.maximum), axis=-1, keepdims=True)
    return m if extra_logit is None else jnp.maximum(m, extra_logit)


def _exp_pv(scores, m, values, extra_logit=None):
    acc = None
    for s, v in zip(scores, values):
        v1 = jnp.concatenate([v, jnp.ones((v.shape[0], LANES), _BF16)], axis=1)
        pv = jnp.dot(jnp.exp(s - m).astype(_BF16), v1, preferred_element_type=_F32)
        acc = pv if acc is None else acc + pv
    if extra_logit is not None:
        acc = jnp.concatenate([acc[:, :LANES], acc[:, LANES:] + jnp.exp(extra_logit - m)], axis=1)
    return acc


def _normalise(acc):
    return acc[:, :LANES] / acc[:, LANES:]


def _fold_lane_groups(blocks, op):
    acc = None
    for s in blocks:
        for k in range(0, s.shape[1], LANES):
            c = s[:, k:k + LANES]
            acc = c if acc is None else op(acc, c)
    return acc


def _emit_staged(n_units, stages, lags):
    state = {}
    for step in range(n_units + lags[-1]):
        for stage, lag in zip(stages, lags):
            u = step - lag
            if 0 <= u < n_units:
                state[u] = stage(u, state.get(u))


def _na_kernel(q_ref, kp_ref, kc_ref, kn_ref, vp_ref, vc_ref, vn_ref, kx_ref, vx_ref,
               tbl_ref, o_ref, kbuf, vbuf):
    rb = pl.program_id(1)
    kbuf[0:NA_QT] = kp_ref[0]
    kbuf[NA_QT:2 * NA_QT] = kc_ref[0]
    kbuf[2 * NA_QT:3 * NA_QT] = kn_ref[0]
    vbuf[0:NA_QT] = vp_ref[0]
    vbuf[NA_QT:2 * NA_QT] = vc_ref[0]
    vbuf[2 * NA_QT:3 * NA_QT] = vn_ref[0]
    lo, m_lo, m_hi = _lane_masks()
    n_win = NA_KH * GRID_W
    n_pairs = NA_HEADS // 2

    def geometry(u):
        j, p = divmod(u, n_pairs)
        r = rb * NA_RB + j
        r0 = jnp.clip(r - NA_KH // 2, 0, ROWS - NA_KH)
        dr0 = r0 - r + (NA_KH - 1)
        off = pl.multiple_of((r0 - rb * NA_RB + NA_RB) * GRID_W, GRID_W)
        return j, p, dr0, off, slice(p * LANES, (p + 1) * LANES)

    def score_stage(u, _):
        j, _, _, off, cols = geometry(u)
        qp = q_ref[0, j * GRID_W:(j + 1) * GRID_W, cols]
        q2 = jnp.concatenate([qp * m_lo, qp * m_hi], axis=0)
        s_nb = lax.dot_general(q2, kbuf[pl.ds(off, n_win), cols], _NT,
                               preferred_element_type=_F32)
        s_cx = lax.dot_general(q2, kx_ref[0, :, cols], _NT, preferred_element_type=_F32)
        return s_nb, s_cx

    def max_stage(u, s):
        _, p, dr0, _, _ = geometry(u)
        s_nb, s_cx = s
        bias = jnp.concatenate([
            jnp.concatenate([tbl_ref[2 * p + hh, pl.ds(dr0 + 2 * k, 1)][0]
                             for k in range(NA_KH // 2)], axis=1)
            for hh in range(2)], axis=0)
        sc = [s_nb + bias, s_cx]
        return sc, _row_max(sc)

    def pv_stage(u, s):
        _, _, _, off, cols = geometry(u)
        sc, m = s
        return _exp_pv(sc, m, [vbuf[pl.ds(off, n_win), cols], vx_ref[0, :, cols]])

    def out_stage(u, s):
        j, _, _, _, cols = geometry(u)
        o2 = _normalise(s)
        o_ref[0, j * GRID_W:(j + 1) * GRID_W, cols] = (
            jnp.where(lo, o2[:GRID_W], o2[GRID_W:]).astype(_BF16))

    _emit_staged(NA_RB * n_pairs, [score_stage, max_stage, pv_stage, out_stage], STAGE_LAGS)


def _na_attention(qa, ka, va, tbl):
    n_rb = ROWS // NA_RB
    blk = lambda f: pl.BlockSpec((1, NA_QT, NA_W), f)
    prev = lambda b, i: (b, jnp.maximum(i - 1, 0), 0)
    cur = lambda b, i: (b, i, 0)
    nxt = lambda b, i: (b, jnp.minimum(i + 1, n_rb - 1), 0)
    ctx = pl.BlockSpec((1, CTX_LEN, NA_W), lambda b, i: (b, SEQ // CTX_LEN, 0))
    return pl.pallas_call(
        _na_kernel,
        grid=(BATCH, n_rb),
        in_specs=[blk(cur), blk(prev), blk(cur), blk(nxt), blk(prev), blk(cur), blk(nxt),
                  ctx, ctx,
                  pl.BlockSpec(tbl.shape, lambda b, i: (0, 0, 0, 0))],
        out_specs=blk(cur),
        out_shape=jax.ShapeDtypeStruct((BATCH, SEQ, NA_W), _BF16),
        scratch_shapes=[pltpu.VMEM((3 * NA_QT, NA_W), _BF16),
                        pltpu.VMEM((3 * NA_QT, NA_W), _BF16)],
        compiler_params=pltpu.CompilerParams(
            dimension_semantics=("arbitrary", "arbitrary"), vmem_limit_bytes=VMEM_LIMIT),
        name="na_attention",
    )(qa, ka, ka, ka, va, va, va, ka, va, tbl)


def _stack_heads(q, m_lo, m_hi):
    parts = []
    for pos in range(WA_HEADS):
        qp = q[:, (pos // 2) * LANES:(pos // 2 + 1) * LANES]
        parts.append(qp * (m_lo if pos % 2 == 0 else m_hi))
    return jnp.concatenate(parts, axis=0)


def _unstack_heads(o, rows, lo):
    outs = []
    for g in range(WA_HEADS // 2):
        outs.append(jnp.where(lo, o[(2 * g) * rows:(2 * g + 1) * rows],
                              o[(2 * g + 1) * rows:(2 * g + 2) * rows]))
    return jnp.concatenate(outs, axis=1)


def _sink_column(sink_ref, rows):
    return jnp.concatenate(
        [jnp.full((rows, 1), sink_ref[pos], _F32) for pos in range(WA_HEADS)], axis=0)


def _wa_kernel(sink_ref, q_ref, kp_ref, kc_ref, kn_ref, vp_ref, vc_ref, vn_ref, kx_ref, vx_ref,
               o_ref, kbuf, vbuf):
    i = pl.program_id(1)
    kbuf[0:WA_BLOCK] = kp_ref[0]
    kbuf[WA_BLOCK:WA_BLOCK + WA_QT] = kc_ref[0]
    kbuf[WA_BLOCK + WA_QT:] = kn_ref[0]
    vbuf[0:WA_BLOCK] = vp_ref[0]
    vbuf[WA_BLOCK:WA_BLOCK + WA_QT] = vc_ref[0]
    vbuf[WA_BLOCK + WA_QT:] = vn_ref[0]
    lo, m_lo, m_hi = _lane_masks()
    n_loc = 3 * WA_BLOCK
    q_off = lax.broadcasted_iota(jnp.int32, (WA_BLOCK, n_loc), 0)
    k_off = lax.broadcasted_iota(jnp.int32, (WA_BLOCK, n_loc), 1) - WA_BLOCK
    band = jnp.abs(k_off - q_off) <= WA_WINDOW
    n_pairs = WA_HEADS // 2

    def geometry(u):
        t, g = divmod(u, n_pairs)
        return t, g, slice(t * WA_BLOCK, (t + 1) * WA_BLOCK), slice(g * LANES, (g + 1) * LANES)

    def sink_col(g):
        return jnp.concatenate(
            [jnp.full((WA_BLOCK, 1), sink_ref[2 * g + hh], _F32) for hh in range(2)], axis=0)

    def score_stage(u, _):
        t, _, rows, cols = geometry(u)
        qp = q_ref[0, rows, cols]
        q2 = jnp.concatenate([qp * m_lo, qp * m_hi], axis=0)
        s_loc = lax.dot_general(q2, kbuf[t * WA_BLOCK:t * WA_BLOCK + n_loc, :], _NT,
                                preferred_element_type=_F32)
        s_cx = lax.dot_general(q2, kx_ref[0], _NT, preferred_element_type=_F32)
        return s_loc, s_cx

    def max_stage(u, s):
        t, g, _, _ = geometry(u)
        s_loc, s_cx = s
        k_pos = (i * WA_SUB + t) * WA_BLOCK + k_off
        valid = band & (k_pos >= 0) & (k_pos < SEQ)
        s_loc = jnp.where(valid[None], s_loc.reshape(2, WA_BLOCK, n_loc), NEG_INF)
        sc = [s_loc.reshape(2 * WA_BLOCK, n_loc), s_cx]
        return sc, _row_max(sc, sink_col(g))

    def pv_stage(u, s):
        t, g, _, _ = geometry(u)
        sc, m = s
        return _exp_pv(sc, m, [vbuf[t * WA_BLOCK:t * WA_BLOCK + n_loc, :], vx_ref[0]],
                       sink_col(g))

    def out_stage(u, s):
        _, _, rows, cols = geometry(u)
        o2 = _normalise(s)
        o_ref[0, rows, cols] = jnp.where(lo, o2[:WA_BLOCK], o2[WA_BLOCK:]).astype(_BF16)

    _emit_staged(WA_SUB * n_pairs, [score_stage, max_stage, pv_stage, out_stage], STAGE_LAGS)


def _wa_attention(qb, kb, vb, sink):
    n_blk = SEQ // WA_BLOCK
    qspec = pl.BlockSpec((1, WA_QT, WA_W), lambda b, i: (b, i, 0))
    cur = pl.BlockSpec((1, WA_QT, WA_KV_W), lambda b, i: (b, i, 0))
    prev = pl.BlockSpec((1, WA_BLOCK, WA_KV_W),
                        lambda b, i: (b, jnp.maximum(i * WA_SUB - 1, 0), 0))
    nxt = pl.BlockSpec((1, WA_BLOCK, WA_KV_W),
                       lambda b, i: (b, jnp.minimum((i + 1) * WA_SUB, n_blk - 1), 0))
    ctx = pl.BlockSpec((1, CTX_LEN, WA_KV_W), lambda b, i: (b, SEQ // CTX_LEN, 0))
    return pl.pallas_call(
        _wa_kernel,
        grid=(BATCH, SEQ // WA_QT),
        in_specs=[pl.BlockSpec(memory_space=pltpu.SMEM),
                  qspec, prev, cur, nxt, prev, cur, nxt, ctx, ctx],
        out_specs=qspec,
        out_shape=jax.ShapeDtypeStruct((BATCH, SEQ, WA_W), _BF16),
        scratch_shapes=[pltpu.VMEM((WA_QT + 2 * WA_BLOCK, WA_KV_W), _BF16),
                        pltpu.VMEM((WA_QT + 2 * WA_BLOCK, WA_KV_W), _BF16)],
        compiler_params=pltpu.CompilerParams(
            dimension_semantics=("arbitrary", "arbitrary"), vmem_limit_bytes=VMEM_LIMIT),
        name="wa_attention",
    )(sink, qb, kb, kb, kb, vb, vb, vb, kb, vb)


def _ctx_kernel(sink_ref, qa_ref, ka_ref, va_ref, qb_ref, kb_ref, vb_ref, oa_ref, ob_ref):
    lo, m_lo, m_hi = _lane_masks()
    qa = qa_ref[0]
    outs = []
    for p in range(NA_HEADS // 2):
        cols = slice(p * LANES, (p + 1) * LANES)
        qp = qa[:, cols]
        q2 = jnp.concatenate([qp * m_lo, qp * m_hi], axis=0)
        s = lax.dot_general(q2, ka_ref[0, :, cols], _NT, preferred_element_type=_F32)
        o2 = _softmax_pv([s], [va_ref[0, :, cols]])
        outs.append(jnp.where(lo, o2[:CTX_LEN], o2[CTX_LEN:]))
    oa_ref[0] = jnp.concatenate(outs, axis=1).astype(_BF16)

    q8 = _stack_heads(qb_ref[0], m_lo, m_hi)
    s = lax.dot_general(q8, kb_ref[0], _NT, preferred_element_type=_F32)
    o8 = _softmax_pv([s], [vb_ref[0]], extra_logit=_sink_column(sink_ref, CTX_LEN))
    ob_ref[0] = _unstack_heads(o8, CTX_LEN, lo).astype(_BF16)


def _ctx_attention(qa, ka, va, qb, kb, vb, sink):
    ctx = lambda w: pl.BlockSpec((1, CTX_LEN, w), lambda b: (b, SEQ // CTX_LEN, 0))
    out = lambda w: pl.BlockSpec((1, CTX_LEN, w), lambda b: (b, 0, 0))
    return pl.pallas_call(
        _ctx_kernel,
        grid=(BATCH,),
        in_specs=[pl.BlockSpec(memory_space=pltpu.SMEM),
                  ctx(NA_W), ctx(NA_W), ctx(NA_W), ctx(WA_W), ctx(WA_KV_W), ctx(WA_KV_W)],
        out_specs=[out(NA_W), out(WA_W)],
        out_shape=[jax.ShapeDtypeStruct((BATCH, CTX_LEN, NA_W), _BF16),
                   jax.ShapeDtypeStruct((BATCH, CTX_LEN, WA_W), _BF16)],
        compiler_params=pltpu.CompilerParams(
            dimension_semantics=("arbitrary",), vmem_limit_bytes=VMEM_LIMIT),
        name="ctx_attention",
    )(sink, qa, ka, va, qb, kb, vb)


def _rope_tables():
    rot = HEAD_DIM // 2
    inv_freq = ROPE_BASE ** (-jnp.arange(0, rot, 2, dtype=_F32) / rot)
    t = jnp.arange(SEQ)
    row = (t // GRID_W).astype(_F32)
    col = (t % GRID_W).astype(_F32)
    ang = jnp.stack([row[:, None] * inv_freq, col[:, None] * inv_freq], axis=1)
    d = np.arange(LANES) % HEAD_DIM
    axis, half, freq = d // rot, (d % rot) // (rot // 2), d % (rot // 2)
    ang_l = ang[:, axis, freq]
    sign = jnp.asarray(np.where(half == 0, -1.0, 1.0), _F32)
    cos_t = jnp.concatenate([jnp.cos(ang_l), jnp.ones((CTX_LEN, LANES), _F32)], axis=0)
    sin_t = jnp.concatenate([jnp.sin(ang_l) * sign, jnp.zeros((CTX_LEN, LANES), _F32)], axis=0)
    return cos_t, sin_t


def _na_bias_table(rpb):
    col = np.arange(GRID_W)
    c0 = np.clip(col - NA_KW // 2, 0, GRID_W - NA_KW)
    in_win = (col[None, :] >= c0[:, None]) & (col[None, :] < c0[:, None] + NA_KW)
    dc = np.clip(col[None, :] - col[:, None] + (NA_KW - 1), 0, 2 * NA_KW - 2)
    full = jnp.where(in_win[None, None], rpb.astype(_F32)[:, :, dc], NEG_INF)
    return jnp.concatenate([full[:, :-1], full[:, 1:]], axis=-1)


def kernel(x, c, ctx, c_ctx, w_ada, b_ada, norm_g, w_ffn_up, w_ffn_down, w_in, w_out,
           qk_norm_g, na_rpb, wa_sink):
    group = WA_HEADS // WA_KV_HEADS
    w_up_b = w_ffn_up.astype(_BF16)
    w_dn_b = w_ffn_down.astype(_BF16)
    w_in_b = w_in.astype(_BF16)
    wq_b = (w_in_b[:, :, NA_W:Q_W].reshape(DEPTH, D_MODEL, WA_KV_HEADS, group, HEAD_DIM)
            .transpose(0, 1, 3, 2, 4).reshape(DEPTH, D_MODEL, WA_W))
    w_in_b = jnp.concatenate([w_in_b[:, :, :NA_W], wq_b, w_in_b[:, :, Q_W:]], axis=2)
    w_out_b = w_out.astype(_BF16)
    wo_b = (w_out_b[:, NA_W:].reshape(DEPTH, WA_KV_HEADS, group, HEAD_DIM, D_MODEL)
            .transpose(0, 2, 1, 3, 4).reshape(DEPTH, WA_W, D_MODEL))
    w_out_b = jnp.concatenate([w_out_b[:, :NA_W], wo_b], axis=1)
    sink_p = (wa_sink.astype(_F32).reshape(DEPTH, WA_KV_HEADS, group)
              .transpose(0, 2, 1).reshape(DEPTH, WA_HEADS))
    g4 = jnp.tile(qk_norm_g.astype(_F32), (1, 1, NA_W // HEAD_DIM))
    blk = np.arange(MXU_TILE) // HEAD_DIM
    bd = jnp.asarray((blk[:, None] == blk[None, :]) / HEAD_DIM, _BF16)
    cos_t, sin_t = _rope_tables()

    cond = jnp.zeros((N_COND, D_MODEL), _F32).at[:BATCH].set(c).at[CTX_COND].set(c_ctx)
    mods = _ada_params(cond, w_ada, b_ada).reshape(DEPTH, N_COND, N_MOD, D_MODEL)

    x_all = (x, ctx)
    for l in range(DEPTH):
        last = l == DEPTH - 1
        mod_l = mods[l]
        x_all = _ffn(x_all, mod_l, norm_g[l, 0:1], w_up_b[l, 0], w_dn_b[l, 0], N_TILES)
        qa, qb, ka, va, kb, vb = _proj(x_all, mod_l, norm_g[l, 1:2], w_in_b[l], g4[l], bd,
                                       cos_t, sin_t)
        attn_a = _na_attention(qa, ka, va, _na_bias_table(na_rpb[l]))
        attn_b = _wa_attention(qb, kb, vb, sink_p[l])
        if last:
            ctx_a = jnp.zeros((BATCH, CTX_LEN, NA_W), _BF16)
            ctx_b = jnp.zeros((BATCH, CTX_LEN, WA_W), _BF16)
        else:
            ctx_a, ctx_b = _ctx_attention(qa, ka, va, qb, kb, vb, sink_p[l])
        x_all = _ffn(x_all, mod_l, norm_g[l, 2:3], w_up_b[l, 1], w_dn_b[l, 1],
                     CTX_TILE if last else N_TILES, mix=(attn_a, attn_b, ctx_a, ctx_b, w_out_b[l]))
    return x_all
```

```python
import functools

import numpy as np
import jax
import jax.numpy as jnp
from jax import lax
from jax.experimental import pallas as pl
from jax.experimental.pallas import tpu as pltpu

D_MODEL = 1024
BATCH = 4
SEQ = 8192
DEPTH = 4
CTX_LEN = 256
GRID_W = 64
HEAD_DIM = 64
N_HEADS = D_MODEL // HEAD_DIM
NA_HEADS = N_HEADS // 2
WA_HEADS = N_HEADS - NA_HEADS
WA_KV_HEADS = max(1, WA_HEADS // 4)
NA_KH = 8
NA_KW = 16
WA_WINDOW = 128
WA_BLOCK = 128
D_FF = ((8 * D_MODEL // 3 + 127) // 128) * 128
ROPE_BASE = 10000.0
N_MOD = 9
MACARON_W = 0.5
NORM_EPS = 1e-6
NEG_INF = -1e30
LOG2E = 1.4426950408889634
NA_W = NA_HEADS * HEAD_DIM
WA_W = WA_HEADS * HEAD_DIM
WA_KV_W = WA_KV_HEADS * HEAD_DIM
Q_W = NA_W + WA_W
IN_W = Q_W + 2 * NA_W + 2 * WA_KV_W

ROWS = SEQ // GRID_W
TOK = SEQ + CTX_LEN
TM = 256
N_TILES = TOK // TM
CTX_TILE = SEQ // TM
N_COND = 8
CTX_COND = BATCH
NA_RB = 8
NA_QT = NA_RB * GRID_W
WA_SUB = 4
WA_QT = WA_SUB * WA_BLOCK
STAGE_LAGS = (0, 1, 2, 3)
LANES = 128
MXU_TILE = 256
VMEM_LIMIT = 56 * 1024 * 1024

_F32 = jnp.float32
_BF16 = jnp.bfloat16
_NT = (((1,), (1,)), ((), ()))


def _sigmoid(v):
    return 1.0 / (1.0 + jnp.exp(-v))


def _norm_mod(x, g, shift, scale):
    ms = jnp.mean(x * x, axis=-1, keepdims=True)
    h = x * lax.rsqrt(ms + NORM_EPS) * g
    return h * (1.0 + scale) + shift


def _cond_index(b, t):
    return jnp.where(t == CTX_TILE, CTX_COND, b)


ADA_TN = 1152


def _ada_kernel(cond_ref, w_ref, b_ref, o_ref):
    cond = cond_ref[...]
    a = (cond * _sigmoid(cond)).astype(_BF16)
    o_ref[0] = jnp.dot(a, w_ref[0].astype(_BF16), preferred_element_type=_F32) + b_ref[0]


def _ada_params(cond, w_ada, b_ada):
    n_out = N_MOD * D_MODEL
    return pl.pallas_call(
        _ada_kernel,
        grid=(DEPTH, n_out // ADA_TN),
        in_specs=[
            pl.BlockSpec((N_COND, D_MODEL), lambda l, n: (0, 0)),
            pl.BlockSpec((1, D_MODEL, ADA_TN), lambda l, n: (l, 0, n)),
            pl.BlockSpec((1, 1, ADA_TN), lambda l, n: (l, 0, n)),
        ],
        out_specs=pl.BlockSpec((1, N_COND, ADA_TN), lambda l, n: (l, 0, n)),
        out_shape=jax.ShapeDtypeStruct((DEPTH, N_COND, n_out), _F32),
        compiler_params=pltpu.CompilerParams(vmem_limit_bytes=VMEM_LIMIT),
        name="ada_params",
    )(cond, w_ada, b_ada.reshape(DEPTH, 1, n_out))


def _swiglu_half_step(x, mod, g, wup_ref, wdn_ref, j):
    h = _norm_mod(x, g, mod[j:j + 1], mod[j + 1:j + 2])
    u = jnp.dot(h.astype(_BF16), wup_ref[0, 0], preferred_element_type=_F32)
    gt = u[:, :D_FF]
    up = u[:, D_FF:]
    a = (gt * _sigmoid(gt) * up).astype(_BF16)
    y = jnp.dot(a, wdn_ref[0, 0], preferred_element_type=_F32)
    return x + (MACARON_W * mod[j + 2:j + 3]) * y


def _ffn_kernel(x_ref, mod_ref, g_ref, wup_ref, wdn_ref, o_ref):
    o_ref[0] = _swiglu_half_step(x_ref[0], mod_ref[0], g_ref[...], wup_ref, wdn_ref, 0)


def _entry_ffn_kernel(xl_ref, xc_ref, mod_ref, g_ref, wup_ref, wdn_ref, o_ref):
    x = jnp.where(pl.program_id(1) == CTX_TILE, xc_ref[0], xl_ref[0])
    o_ref[0] = _swiglu_half_step(x, mod_ref[0], g_ref[...], wup_ref, wdn_ref, 0)


def _mix_ffn_kernel(x_ref, mod_ref, g_ref, wup_ref, wdn_ref, al_ref, bl_ref, ac_ref, bc_ref,
                    wo_ref, o_ref):
    is_ctx = pl.program_id(1) == CTX_TILE
    aa = jnp.where(is_ctx, ac_ref[0], al_ref[0])
    ab = jnp.where(is_ctx, bc_ref[0], bl_ref[0])
    y = (jnp.dot(aa, wo_ref[0, :NA_W], preferred_element_type=_F32)
         + jnp.dot(ab, wo_ref[0, NA_W:], preferred_element_type=_F32))
    mod = mod_ref[0]
    x = x_ref[0] + mod[5:6] * y
    o_ref[0] = _swiglu_half_step(x, mod, g_ref[...], wup_ref, wdn_ref, 6)


def _ffn(x_in, mod_l, g, w_up, w_dn, l, half, n_tiles, mix=None):
    const = pl.Buffered(1)
    tok = lambda w: pl.BlockSpec((1, TM, w), lambda b, t: (b, t, 0))
    lat = lambda w: pl.BlockSpec((1, TM, w), lambda b, t: (b, jnp.minimum(t, CTX_TILE - 1), 0))
    ctx = lambda w: pl.BlockSpec((1, CTX_LEN, w), lambda b, t: (b, 0, 0))
    entry = isinstance(x_in, tuple)
    assert not (entry and mix is not None)
    in_specs = ([lat(D_MODEL), ctx(D_MODEL)] if entry else [tok(D_MODEL)]) + [
        pl.BlockSpec((1, N_MOD, D_MODEL), lambda b, t: (_cond_index(b, t), 0, 0)),
        pl.BlockSpec((1, D_MODEL), lambda b, t: (0, 0)),
        pl.BlockSpec((1, 1, D_MODEL, 2 * D_FF), lambda b, t: (l, half, 0, 0), pipeline_mode=const),
        pl.BlockSpec((1, 1, D_FF, D_MODEL), lambda b, t: (l, half, 0, 0), pipeline_mode=const),
    ]
    args = (list(x_in) if entry else [x_in]) + [mod_l, g, w_up, w_dn]
    kern, name = (_entry_ffn_kernel, "entry_ffn") if entry else (_ffn_kernel, "ffn")
    if mix is not None:
        kern, name = _mix_ffn_kernel, "mix_ffn"
        in_specs += [lat(NA_W), lat(WA_W), ctx(NA_W), ctx(WA_W),
                     pl.BlockSpec((1, Q_W, D_MODEL), lambda b, t: (l, 0, 0), pipeline_mode=const)]
        args += list(mix)
    return pl.pallas_call(
        kern,
        grid=(BATCH, n_tiles),
        in_specs=in_specs,
        out_specs=tok(D_MODEL),
        out_shape=jax.ShapeDtypeStruct((BATCH, n_tiles * TM, D_MODEL), _F32),
        compiler_params=pltpu.CompilerParams(
            dimension_semantics=("arbitrary", "arbitrary"), vmem_limit_bytes=VMEM_LIMIT),
        name=name,
    )(*args)


def _head_norm(t, g, bd):
    width = t.shape[1]
    outs = []
    for k in range(0, width, MXU_TILE):
        w = min(MXU_TILE, width - k)
        tc = t[:, k:k + w]
        ms = jnp.dot((tc * tc).astype(_BF16), bd[:w, :w], preferred_element_type=_F32)
        outs.append(tc * lax.rsqrt(ms + NORM_EPS))
    out = outs[0] if len(outs) == 1 else jnp.concatenate(outs, axis=-1)
    return out * g


def _rope(t, cos, sin_signed, first_half):
    outs = []
    for k in range(0, t.shape[1], LANES):
        tc = t[:, k:k + LANES]
        partner = jnp.where(first_half,
                            pltpu.roll(tc, LANES - HEAD_DIM // 4, 1),
                            pltpu.roll(tc, HEAD_DIM // 4, 1))
        outs.append(tc * cos + partner * sin_signed)
    return outs[0] if len(outs) == 1 else jnp.concatenate(outs, axis=-1)


def _proj_kernel(x_ref, mod_ref, g_ref, win_ref, g4_ref, bd_ref, cos_ref, sin_ref,
                 qa_ref, qb_ref, ka_ref, va_ref, kb_ref, vb_ref):
    x = x_ref[0]
    mod = mod_ref[0]
    h = _norm_mod(x, g_ref[...], mod[3:4], mod[4:5])
    px = jnp.dot(h.astype(_BF16), win_ref[0], preferred_element_type=_F32)
    bd = bd_ref[...]
    g4 = g4_ref[...]
    cos = cos_ref[...]
    sin = sin_ref[...]
    lane = lax.broadcasted_iota(jnp.int32, (1, LANES), 1)
    first_half = (lane % (HEAD_DIM // 2)) < (HEAD_DIM // 4)
    q_scale = HEAD_DIM ** -0.5 * LOG2E

    o1, o2, o3 = Q_W, Q_W + NA_W, Q_W + 2 * NA_W
    o4 = o3 + WA_KV_W
    qa = _head_norm(px[:, :NA_W], g4[0:1, :NA_W], bd)
    qa_ref[0] = (qa * q_scale).astype(_BF16)
    qb = _rope(_head_norm(px[:, NA_W:Q_W], g4[2:3, :WA_W], bd), cos, sin, first_half)
    qb_ref[0] = (qb * q_scale).astype(_BF16)
    ka = _head_norm(px[:, o1:o2], g4[1:2, :NA_W], bd)
    ka_ref[0] = ka.astype(_BF16)
    va_ref[0] = px[:, o2:o3].astype(_BF16)
    kb = _rope(_head_norm(px[:, o3:o4], g4[3:4, :WA_KV_W], bd), cos, sin, first_half)
    kb_ref[0] = kb.astype(_BF16)
    vb_ref[0] = px[:, o4:].astype(_BF16)


def _proj(x_all, mod_l, g, w_in, l, g4, bd, cos_t, sin_t):
    const = pl.Buffered(1)
    tok = lambda w: pl.BlockSpec((1, TM, w), lambda b, t: (b, t, 0))
    shp = lambda w: jax.ShapeDtypeStruct((BATCH, TOK, w), _BF16)
    return pl.pallas_call(
        _proj_kernel,
        grid=(BATCH, N_TILES),
        in_specs=[
            tok(D_MODEL),
            pl.BlockSpec((1, N_MOD, D_MODEL), lambda b, t: (_cond_index(b, t), 0, 0)),
            pl.BlockSpec((1, D_MODEL), lambda b, t: (0, 0)),
            pl.BlockSpec((1, D_MODEL, IN_W), lambda b, t: (l, 0, 0), pipeline_mode=const),
            pl.BlockSpec((4, NA_W), lambda b, t: (0, 0)),
            pl.BlockSpec((MXU_TILE, MXU_TILE), lambda b, t: (0, 0)),
            pl.BlockSpec((TM, LANES), lambda b, t: (t, 0)),
            pl.BlockSpec((TM, LANES), lambda b, t: (t, 0)),
        ],
        out_specs=[tok(NA_W), tok(WA_W), tok(NA_W), tok(NA_W), tok(WA_KV_W), tok(WA_KV_W)],
        out_shape=[shp(NA_W), shp(WA_W), shp(NA_W), shp(NA_W), shp(WA_KV_W), shp(WA_KV_W)],
        compiler_params=pltpu.CompilerParams(
            dimension_semantics=("arbitrary", "arbitrary"), vmem_limit_bytes=VMEM_LIMIT),
        name="qkv_proj",
    )(x_all, mod_l, g, w_in, g4, bd, cos_t, sin_t)


def _lane_masks():
    lane = lax.broadcasted_iota(jnp.int32, (1, LANES), 1)
    lo = lane < HEAD_DIM
    return lo, lo.astype(_BF16), (~lo).astype(_BF16)


def _softmax_pv(scores, values, extra_logit=None):
    m = _row_max(scores, extra_logit)
    return _normalise(_exp_pv(scores, m, values, extra_logit))


def _row_max(scores, extra_logit=None):
    m = jnp.max(_fold_lane_groups(scores, jnp.maximum), axis=-1, keepdims=True)
    return m if extra_logit is None else jnp.maximum(m, extra_logit)


def _exp_pv(scores, m, values, extra_logit=None):
    acc = None
    for s, v in zip(scores, values):
        v1 = jnp.concatenate([v, jnp.ones((v.shape[0], LANES), _BF16)], axis=1)
        pv = jnp.dot(jnp.exp2(s - m).astype(_BF16), v1, preferred_element_type=_F32)
        acc = pv if acc is None else acc + pv
    if extra_logit is not None:
        acc = jnp.concatenate([acc[:, :LANES], acc[:, LANES:] + jnp.exp2(extra_logit - m)], axis=1)
    return acc


def _normalise(acc):
    return acc[:, :LANES] / acc[:, LANES:]


def _fold_lane_groups(blocks, op):
    acc = None
    for s in blocks:
        for k in range(0, s.shape[1], LANES):
            c = s[:, k:k + LANES]
            acc = c if acc is None else op(acc, c)
    return acc


def _emit_staged(n_units, stages, lags):
    state = {}
    for step in range(n_units + lags[-1]):
        for stage, lag in zip(stages, lags):
            u = step - lag
            if 0 <= u < n_units:
                state[u] = stage(u, state.get(u))


def _na_kernel(q_ref, kp_ref, kc_ref, kn_ref, vp_ref, vc_ref, vn_ref, kx_ref, vx_ref,
               tbl_ref, o_ref, kbuf, vbuf):
    rb = pl.program_id(1)
    kbuf[0:NA_QT] = kp_ref[0]
    kbuf[NA_QT:2 * NA_QT] = kc_ref[0]
    kbuf[2 * NA_QT:3 * NA_QT] = kn_ref[0]
    vbuf[0:NA_QT] = vp_ref[0]
    vbuf[NA_QT:2 * NA_QT] = vc_ref[0]
    vbuf[2 * NA_QT:3 * NA_QT] = vn_ref[0]
    lo, m_lo, m_hi = _lane_masks()
    n_win = NA_KH * GRID_W
    n_pairs = NA_HEADS // 2

    def geometry(u):
        j, p = divmod(u, n_pairs)
        r = rb * NA_RB + j
        r0 = jnp.clip(r - NA_KH // 2, 0, ROWS - NA_KH)
        dr0 = r0 - r + (NA_KH - 1)
        off = pl.multiple_of((r0 - rb * NA_RB + NA_RB) * GRID_W, GRID_W)
        return j, p, dr0, off, slice(p * LANES, (p + 1) * LANES)

    def score_stage(u, _):
        j, _, _, off, cols = geometry(u)
        qp = q_ref[0, j * GRID_W:(j + 1) * GRID_W, cols]
        q2 = jnp.concatenate([qp * m_lo, qp * m_hi], axis=0)
        s_nb = lax.dot_general(q2, kbuf[pl.ds(off, n_win), cols], _NT,
                               preferred_element_type=_F32)
        s_cx = lax.dot_general(q2, kx_ref[0, :, cols], _NT, preferred_element_type=_F32)
        return s_nb, s_cx

    def max_stage(u, s):
        _, p, dr0, _, _ = geometry(u)
        s_nb, s_cx = s
        bias = jnp.concatenate([
            jnp.concatenate([tbl_ref[0, 2 * p + hh, pl.ds(dr0 + 2 * k, 1)][0]
                             for k in range(NA_KH // 2)], axis=1)
            for hh in range(2)], axis=0)
        sc = [s_nb + bias, s_cx]
        return sc, _row_max(sc)

    def pv_stage(u, s):
        _, _, _, off, cols = geometry(u)
        sc, m = s
        return _exp_pv(sc, m, [vbuf[pl.ds(off, n_win), cols], vx_ref[0, :, cols]])

    def out_stage(u, s):
        j, _, _, _, cols = geometry(u)
        o2 = _normalise(s)
        o_ref[0, j * GRID_W:(j + 1) * GRID_W, cols] = (
            jnp.where(lo, o2[:GRID_W], o2[GRID_W:]).astype(_BF16))

    _emit_staged(NA_RB * n_pairs, [score_stage, max_stage, pv_stage, out_stage], STAGE_LAGS)


def _na_attention(qa, ka, va, tbl, l):
    n_rb = ROWS // NA_RB
    blk = lambda f: pl.BlockSpec((1, NA_QT, NA_W), f)
    prev = lambda b, i: (b, jnp.maximum(i - 1, 0), 0)
    cur = lambda b, i: (b, i, 0)
    nxt = lambda b, i: (b, jnp.minimum(i + 1, n_rb - 1), 0)
    ctx = pl.BlockSpec((1, CTX_LEN, NA_W), lambda b, i: (b, SEQ // CTX_LEN, 0))
    return pl.pallas_call(
        _na_kernel,
        grid=(BATCH, n_rb),
        in_specs=[blk(cur), blk(prev), blk(cur), blk(nxt), blk(prev), blk(cur), blk(nxt),
                  ctx, ctx,
                  pl.BlockSpec((1,) + tbl.shape[1:], lambda b, i: (l, 0, 0, 0, 0))],
        out_specs=blk(cur),
        out_shape=jax.ShapeDtypeStruct((BATCH, SEQ, NA_W), _BF16),
        scratch_shapes=[pltpu.VMEM((3 * NA_QT, NA_W), _BF16),
                        pltpu.VMEM((3 * NA_QT, NA_W), _BF16)],
        compiler_params=pltpu.CompilerParams(
            dimension_semantics=("arbitrary", "arbitrary"), vmem_limit_bytes=VMEM_LIMIT),
        name="na_attention",
    )(qa, ka, ka, ka, va, va, va, ka, va, tbl)


def _stack_heads(q, m_lo, m_hi):
    parts = []
    for pos in range(WA_HEADS):
        qp = q[:, (pos // 2) * LANES:(pos // 2 + 1) * LANES]
        parts.append(qp * (m_lo if pos % 2 == 0 else m_hi))
    return jnp.concatenate(parts, axis=0)


def _unstack_heads(o, rows, lo):
    outs = []
    for g in range(WA_HEADS // 2):
        outs.append(jnp.where(lo, o[(2 * g) * rows:(2 * g + 1) * rows],
                              o[(2 * g + 1) * rows:(2 * g + 2) * rows]))
    return jnp.concatenate(outs, axis=1)


def _sink_column(sink_ref, rows):
    return jnp.concatenate(
        [jnp.full((rows, 1), sink_ref[pos], _F32) for pos in range(WA_HEADS)], axis=0)


def _wa_kernel(sink_ref, q_ref, kp_ref, kc_ref, kn_ref, vp_ref, vc_ref, vn_ref, kx_ref, vx_ref,
               o_ref, kbuf, vbuf):
    i = pl.program_id(1)
    kbuf[0:WA_BLOCK] = kp_ref[0]
    kbuf[WA_BLOCK:WA_BLOCK + WA_QT] = kc_ref[0]
    kbuf[WA_BLOCK + WA_QT:] = kn_ref[0]
    vbuf[0:WA_BLOCK] = vp_ref[0]
    vbuf[WA_BLOCK:WA_BLOCK + WA_QT] = vc_ref[0]
    vbuf[WA_BLOCK + WA_QT:] = vn_ref[0]
    lo, m_lo, m_hi = _lane_masks()
    n_loc = 3 * WA_BLOCK
    q_off = lax.broadcasted_iota(jnp.int32, (WA_BLOCK, n_loc), 0)
    k_off = lax.broadcasted_iota(jnp.int32, (WA_BLOCK, n_loc), 1) - WA_BLOCK
    band = jnp.abs(k_off - q_off) <= WA_WINDOW
    n_pairs = WA_HEADS // 2

    def geometry(u):
        t, g = divmod(u, n_pairs)
        return t, g, slice(t * WA_BLOCK, (t + 1) * WA_BLOCK), slice(g * LANES, (g + 1) * LANES)

    def sink_col(g):
        return jnp.concatenate(
            [jnp.full((WA_BLOCK, 1), sink_ref[2 * g + hh], _F32) for hh in range(2)], axis=0)

    def score_stage(u, _):
        t, _, rows, cols = geometry(u)
        qp = q_ref[0, rows, cols]
        q2 = jnp.concatenate([qp * m_lo, qp * m_hi], axis=0)
        s_loc = lax.dot_general(q2, kbuf[t * WA_BLOCK:t * WA_BLOCK + n_loc, :], _NT,
                                preferred_element_type=_F32)
        s_cx = lax.dot_general(q2, kx_ref[0], _NT, preferred_element_type=_F32)
        return s_loc, s_cx

    def max_stage(u, s):
        t, g, _, _ = geometry(u)
        s_loc, s_cx = s
        k_pos = (i * WA_SUB + t) * WA_BLOCK + k_off
        valid = band & (k_pos >= 0) & (k_pos < SEQ)
        s_loc = jnp.where(valid[None], s_loc.reshape(2, WA_BLOCK, n_loc), NEG_INF)
        sc = [s_loc.reshape(2 * WA_BLOCK, n_loc), s_cx]
        return sc, _row_max(sc, sink_col(g))

    def pv_stage(u, s):
        t, g, _, _ = geometry(u)
        sc, m = s
        return _exp_pv(sc, m, [vbuf[t * WA_BLOCK:t * WA_BLOCK + n_loc, :], vx_ref[0]],
                       sink_col(g))

    def out_stage(u, s):
        _, _, rows, cols = geometry(u)
        o2 = _normalise(s)
        o_ref[0, rows, cols] = jnp.where(lo, o2[:WA_BLOCK], o2[WA_BLOCK:]).astype(_BF16)

    _emit_staged(WA_SUB * n_pairs, [score_stage, max_stage, pv_stage, out_stage], STAGE_LAGS)


def _wa_attention(qb, kb, vb, sink):
    n_blk = SEQ // WA_BLOCK
    qspec = pl.BlockSpec((1, WA_QT, WA_W), lambda b, i: (b, i, 0))
    cur = pl.BlockSpec((1, WA_QT, WA_KV_W), lambda b, i: (b, i, 0))
    prev = pl.BlockSpec((1, WA_BLOCK, WA_KV_W),
                        lambda b, i: (b, jnp.maximum(i * WA_SUB - 1, 0), 0))
    nxt = pl.BlockSpec((1, WA_BLOCK, WA_KV_W),
                       lambda b, i: (b, jnp.minimum((i + 1) * WA_SUB, n_blk - 1), 0))
    ctx = pl.BlockSpec((1, CTX_LEN, WA_KV_W), lambda b, i: (b, SEQ // CTX_LEN, 0))
    return pl.pallas_call(
        _wa_kernel,
        grid=(BATCH, SEQ // WA_QT),
        in_specs=[pl.BlockSpec(memory_space=pltpu.SMEM),
                  qspec, prev, cur, nxt, prev, cur, nxt, ctx, ctx],
        out_specs=qspec,
        out_shape=jax.ShapeDtypeStruct((BATCH, SEQ, WA_W), _BF16),
        scratch_shapes=[pltpu.VMEM((WA_QT + 2 * WA_BLOCK, WA_KV_W), _BF16),
                        pltpu.VMEM((WA_QT + 2 * WA_BLOCK, WA_KV_W), _BF16)],
        compiler_params=pltpu.CompilerParams(
            dimension_semantics=("arbitrary", "arbitrary"), vmem_limit_bytes=VMEM_LIMIT),
        name="wa_attention",
    )(sink, qb, kb, kb, kb, vb, vb, vb, kb, vb)


def _ctx_kernel(sink_ref, qa_ref, ka_ref, va_ref, qb_ref, kb_ref, vb_ref, oa_ref, ob_ref):
    lo, m_lo, m_hi = _lane_masks()
    qa = qa_ref[0]
    outs = []
    for p in range(NA_HEADS // 2):
        cols = slice(p * LANES, (p + 1) * LANES)
        qp = qa[:, cols]
        q2 = jnp.concatenate([qp * m_lo, qp * m_hi], axis=0)
        s = lax.dot_general(q2, ka_ref[0, :, cols], _NT, preferred_element_type=_F32)
        o2 = _softmax_pv([s], [va_ref[0, :, cols]])
        outs.append(jnp.where(lo, o2[:CTX_LEN], o2[CTX_LEN:]))
    oa_ref[0] = jnp.concatenate(outs, axis=1).astype(_BF16)

    q8 = _stack_heads(qb_ref[0], m_lo, m_hi)
    s = lax.dot_general(q8, kb_ref[0], _NT, preferred_element_type=_F32)
    o8 = _softmax_pv([s], [vb_ref[0]], extra_logit=_sink_column(sink_ref, CTX_LEN))
    ob_ref[0] = _unstack_heads(o8, CTX_LEN, lo).astype(_BF16)


def _ctx_attention(qa, ka, va, qb, kb, vb, sink):
    ctx = lambda w: pl.BlockSpec((1, CTX_LEN, w), lambda b: (b, SEQ // CTX_LEN, 0))
    out = lambda w: pl.BlockSpec((1, CTX_LEN, w), lambda b: (b, 0, 0))
    return pl.pallas_call(
        _ctx_kernel,
        grid=(BATCH,),
        in_specs=[pl.BlockSpec(memory_space=pltpu.SMEM),
                  ctx(NA_W), ctx(NA_W), ctx(NA_W), ctx(WA_W), ctx(WA_KV_W), ctx(WA_KV_W)],
        out_specs=[out(NA_W), out(WA_W)],
        out_shape=[jax.ShapeDtypeStruct((BATCH, CTX_LEN, NA_W), _BF16),
                   jax.ShapeDtypeStruct((BATCH, CTX_LEN, WA_W), _BF16)],
        compiler_params=pltpu.CompilerParams(
            dimension_semantics=("arbitrary",), vmem_limit_bytes=VMEM_LIMIT),
        name="ctx_attention",
    )(sink, qa, ka, va, qb, kb, vb)


def _rope_tables():
    rot = HEAD_DIM // 2
    inv_freq = ROPE_BASE ** (-jnp.arange(0, rot, 2, dtype=_F32) / rot)
    t = jnp.arange(SEQ)
    row = (t // GRID_W).astype(_F32)
    col = (t % GRID_W).astype(_F32)
    ang = jnp.stack([row[:, None] * inv_freq, col[:, None] * inv_freq], axis=1)
    d = np.arange(LANES) % HEAD_DIM
    axis, half, freq = d // rot, (d % rot) // (rot // 2), d % (rot // 2)
    ang_l = ang[:, axis, freq]
    sign = jnp.asarray(np.where(half == 0, -1.0, 1.0), _F32)
    cos_t = jnp.concatenate([jnp.cos(ang_l), jnp.ones((CTX_LEN, LANES), _F32)], axis=0)
    sin_t = jnp.concatenate([jnp.sin(ang_l) * sign, jnp.zeros((CTX_LEN, LANES), _F32)], axis=0)
    return cos_t, sin_t


def _na_bias_tables(rpb):
    col = np.arange(GRID_W)
    c0 = np.clip(col - NA_KW // 2, 0, GRID_W - NA_KW)
    in_win = (col[None, :] >= c0[:, None]) & (col[None, :] < c0[:, None] + NA_KW)
    dc = np.clip(col[None, :] - col[:, None] + (NA_KW - 1), 0, 2 * NA_KW - 2)
    onehot = jnp.asarray(np.arange(2 * NA_KW - 1)[:, None] == dc.reshape(1, -1), _F32)
    full = jnp.dot(rpb.astype(_F32).reshape(-1, 2 * NA_KW - 1), onehot,
                   precision=lax.Precision.HIGHEST)
    full = full.reshape(DEPTH, NA_HEADS, 2 * NA_KH - 1, GRID_W, GRID_W)
    full = jnp.where(in_win, full * LOG2E, NEG_INF)
    return jnp.concatenate([full[:, :, :-1], full[:, :, 1:]], axis=-1)


def kernel(x, c, ctx, c_ctx, w_ada, b_ada, norm_g, w_ffn_up, w_ffn_down, w_in, w_out,
           qk_norm_g, na_rpb, wa_sink):
    group = WA_HEADS // WA_KV_HEADS
    w_up_b = w_ffn_up.astype(_BF16)
    w_dn_b = w_ffn_down.astype(_BF16)
    w_in_b = w_in.astype(_BF16)
    wq_b = (w_in_b[:, :, NA_W:Q_W].reshape(DEPTH, D_MODEL, WA_KV_HEADS, group, HEAD_DIM)
            .transpose(0, 1, 3, 2, 4).reshape(DEPTH, D_MODEL, WA_W))
    w_in_b = jnp.concatenate([w_in_b[:, :, :NA_W], wq_b, w_in_b[:, :, Q_W:]], axis=2)
    w_out_b = w_out.astype(_BF16)
    wo_b = (w_out_b[:, NA_W:].reshape(DEPTH, WA_KV_HEADS, group, HEAD_DIM, D_MODEL)
            .transpose(0, 2, 1, 3, 4).reshape(DEPTH, WA_W, D_MODEL))
    w_out_b = jnp.concatenate([w_out_b[:, :NA_W], wo_b], axis=1)
    sink_p = ((wa_sink.astype(_F32) * LOG2E).reshape(DEPTH, WA_KV_HEADS, group)
              .transpose(0, 2, 1).reshape(DEPTH, WA_HEADS))
    g4 = jnp.tile(qk_norm_g.astype(_F32), (1, 1, NA_W // HEAD_DIM))
    blk = np.arange(MXU_TILE) // HEAD_DIM
    bd = jnp.asarray((blk[:, None] == blk[None, :]) / HEAD_DIM, _BF16)
    cos_t, sin_t = _rope_tables()
    na_tbl = _na_bias_tables(na_rpb)

    cond = jnp.zeros((N_COND, D_MODEL), _F32).at[:BATCH].set(c).at[CTX_COND].set(c_ctx)
    mods = _ada_params(cond, w_ada, b_ada).reshape(DEPTH, N_COND, N_MOD, D_MODEL)

    x_all = (x, ctx)
    for l in range(DEPTH):
        last = l == DEPTH - 1
        mod_l = mods[l]
        x_all = _ffn(x_all, mod_l, norm_g[l, 0:1], w_up_b, w_dn_b, l, 0, N_TILES)
        qa, qb, ka, va, kb, vb = _proj(x_all, mod_l, norm_g[l, 1:2], w_in_b, l, g4[l], bd,
                                       cos_t, sin_t)
        attn_a = _na_attention(qa, ka, va, na_tbl, l)
        attn_b = _wa_attention(qb, kb, vb, sink_p[l])
        if last:
            ctx_a = jnp.zeros((BATCH, CTX_LEN, NA_W), _BF16)
            ctx_b = jnp.zeros((BATCH, CTX_LEN, WA_W), _BF16)
        else:
            ctx_a, ctx_b = _ctx_attention(qa, ka, va, qb, kb, vb, sink_p[l])
        x_all = _ffn(x_all, mod_l, norm_g[l, 2:3], w_up_b, w_dn_b, l, 1,
                     CTX_TILE if last else N_TILES, mix=(attn_a, attn_b, ctx_a, ctx_b, w_out_b))
    return x_all
```

```python
import functools

import numpy as np
import jax
import jax.numpy as jnp
from jax import lax
from jax.experimental import pallas as pl
from jax.experimental.pallas import tpu as pltpu

D_MODEL = 1024
BATCH = 4
SEQ = 8192
DEPTH = 4
CTX_LEN = 256
GRID_W = 64
HEAD_DIM = 64
N_HEADS = D_MODEL // HEAD_DIM
NA_HEADS = N_HEADS // 2
WA_HEADS = N_HEADS - NA_HEADS
WA_KV_HEADS = max(1, WA_HEADS // 4)
NA_KH = 8
NA_KW = 16
WA_WINDOW = 128
WA_BLOCK = 128
D_FF = ((8 * D_MODEL // 3 + 127) // 128) * 128
ROPE_BASE = 10000.0
N_MOD = 9
MACARON_W = 0.5
NORM_EPS = 1e-6
NEG_INF = -1e30
LOG2E = 1.4426950408889634
NA_W = NA_HEADS * HEAD_DIM
WA_W = WA_HEADS * HEAD_DIM
WA_KV_W = WA_KV_HEADS * HEAD_DIM
Q_W = NA_W + WA_W
IN_W = Q_W + 2 * NA_W + 2 * WA_KV_W

ROWS = SEQ // GRID_W
TOK = SEQ + CTX_LEN
TM = 256
N_TILES = TOK // TM
CTX_TILE = SEQ // TM
N_COND = 8
CTX_COND = BATCH
NA_RB = 8
NA_QT = NA_RB * GRID_W
WA_SUB = 4
WA_QT = WA_SUB * WA_BLOCK
STAGE_LAGS = (0, 1, 2, 3)
LANES = 128
MXU_TILE = 256
VMEM_LIMIT = 56 * 1024 * 1024

_F32 = jnp.float32
_BF16 = jnp.bfloat16
_NT = (((1,), (1,)), ((), ()))


def _sigmoid(v):
    return 1.0 / (1.0 + jnp.exp(-v))


def _norm_mod(x, g, shift, scale):
    ms = jnp.mean(x * x, axis=-1, keepdims=True)
    h = x * lax.rsqrt(ms + NORM_EPS) * g
    return h * (1.0 + scale) + shift


def _cond_index(b, t):
    return jnp.where(t == CTX_TILE, CTX_COND, b)


ADA_TN = 1152


def _ada_kernel(cond_ref, w_ref, b_ref, o_ref):
    cond = cond_ref[...]
    a = (cond * _sigmoid(cond)).astype(_BF16)
    o_ref[0] = jnp.dot(a, w_ref[0].astype(_BF16), preferred_element_type=_F32) + b_ref[0]


def _ada_params(cond, w_ada, b_ada):
    n_out = N_MOD * D_MODEL
    return pl.pallas_call(
        _ada_kernel,
        grid=(DEPTH, n_out // ADA_TN),
        in_specs=[
            pl.BlockSpec((N_COND, D_MODEL), lambda l, n: (0, 0)),
            pl.BlockSpec((1, D_MODEL, ADA_TN), lambda l, n: (l, 0, n)),
            pl.BlockSpec((1, 1, ADA_TN), lambda l, n: (l, 0, n)),
        ],
        out_specs=pl.BlockSpec((1, N_COND, ADA_TN), lambda l, n: (l, 0, n)),
        out_shape=jax.ShapeDtypeStruct((DEPTH, N_COND, n_out), _F32),
        compiler_params=pltpu.CompilerParams(vmem_limit_bytes=VMEM_LIMIT),
        name="ada_params",
    )(cond, w_ada, b_ada.reshape(DEPTH, 1, n_out))


def _swiglu_half_step(x, mod, g, wup_ref, wdn_ref, j):
    h = _norm_mod(x, g, mod[j:j + 1], mod[j + 1:j + 2])
    u = jnp.dot(h.astype(_BF16), wup_ref[0, 0], preferred_element_type=_F32)
    gt = u[:, :D_FF]
    up = u[:, D_FF:]
    a = (gt * _sigmoid(gt) * up).astype(_BF16)
    y = jnp.dot(a, wdn_ref[0, 0], preferred_element_type=_F32)
    return x + (MACARON_W * mod[j + 2:j + 3]) * y


def _pre_mix(x, mod_ref, g_ref, wup_ref, wdn_ref, proj_refs, o_ref, qkv_refs):
    mod = mod_ref[0]
    x = _swiglu_half_step(x, mod, g_ref[...], wup_ref, wdn_ref, 0)
    o_ref[0] = x
    _project(x, mod, *proj_refs, *qkv_refs)


def _ffn_kernel(x_ref, mod_ref, g_ref, wup_ref, wdn_ref, *refs):
    _pre_mix(x_ref[0], mod_ref, g_ref, wup_ref, wdn_ref, refs[:N_PROJ_IN], refs[N_PROJ_IN],
             refs[N_PROJ_IN + 1:])


def _entry_ffn_kernel(xl_ref, xc_ref, mod_ref, g_ref, wup_ref, wdn_ref, *refs):
    x = jnp.where(pl.program_id(1) == CTX_TILE, xc_ref[0], xl_ref[0])
    _pre_mix(x, mod_ref, g_ref, wup_ref, wdn_ref, refs[:N_PROJ_IN], refs[N_PROJ_IN],
             refs[N_PROJ_IN + 1:])


def _mix_ffn_kernel(x_ref, mod_ref, g_ref, wup_ref, wdn_ref, al_ref, bl_ref, ac_ref, bc_ref,
                    wo_ref, o_ref):
    is_ctx = pl.program_id(1) == CTX_TILE
    aa = jnp.where(is_ctx, ac_ref[0], al_ref[0])
    ab = jnp.where(is_ctx, bc_ref[0], bl_ref[0])
    y = (jnp.dot(aa, wo_ref[0, :NA_W], preferred_element_type=_F32)
         + jnp.dot(ab, wo_ref[0, NA_W:], preferred_element_type=_F32))
    mod = mod_ref[0]
    x = x_ref[0] + mod[5:6] * y
    o_ref[0] = _swiglu_half_step(x, mod, g_ref[...], wup_ref, wdn_ref, 6)


def _ffn(x_in, mod_l, g, w_up, w_dn, l, half, n_tiles, proj=None, mix=None):
    const = pl.Buffered(1)
    tok = lambda w: pl.BlockSpec((1, TM, w), lambda b, t: (b, t, 0))
    lat = lambda w: pl.BlockSpec((1, TM, w), lambda b, t: (b, jnp.minimum(t, CTX_TILE - 1), 0))
    ctx = lambda w: pl.BlockSpec((1, CTX_LEN, w), lambda b, t: (b, 0, 0))
    entry = isinstance(x_in, tuple)
    assert (proj is None) != (mix is None) and not (entry and proj is None)
    in_specs = ([lat(D_MODEL), ctx(D_MODEL)] if entry else [tok(D_MODEL)]) + [
        pl.BlockSpec((1, N_MOD, D_MODEL), lambda b, t: (_cond_index(b, t), 0, 0)),
        pl.BlockSpec((1, D_MODEL), lambda b, t: (0, 0)),
        pl.BlockSpec((1, 1, D_MODEL, 2 * D_FF), lambda b, t: (l, half, 0, 0), pipeline_mode=const),
        pl.BlockSpec((1, 1, D_FF, D_MODEL), lambda b, t: (l, half, 0, 0), pipeline_mode=const),
    ]
    args = (list(x_in) if entry else [x_in]) + [mod_l, g, w_up, w_dn]
    out_specs = [tok(D_MODEL)]
    out_shape = [jax.ShapeDtypeStruct((BATCH, n_tiles * TM, D_MODEL), _F32)]
    if mix is not None:
        kern, name = _mix_ffn_kernel, "mix_ffn"
        in_specs += [lat(NA_W), lat(WA_W), ctx(NA_W), ctx(WA_W),
                     pl.BlockSpec((1, Q_W, D_MODEL), lambda b, t: (l, 0, 0), pipeline_mode=const)]
        args += list(mix)
    else:
        kern, name = (_entry_ffn_kernel, "entry_ffn") if entry else (_ffn_kernel, "ffn")
        in_specs += [
            pl.BlockSpec((1, D_MODEL), lambda b, t: (0, 0)),
            pl.BlockSpec((1, D_MODEL, IN_W), lambda b, t: (l, 0, 0), pipeline_mode=const),
            pl.BlockSpec((4, NA_W), lambda b, t: (0, 0)),
            pl.BlockSpec((MXU_TILE, MXU_TILE), lambda b, t: (0, 0)),
            pl.BlockSpec((TM, LANES), lambda b, t: (t, 0)),
            pl.BlockSpec((TM, LANES), lambda b, t: (t, 0)),
        ]
        assert len(proj) == N_PROJ_IN
        args += list(proj)
        for w in (NA_W, WA_W, NA_W, NA_W, WA_KV_W, WA_KV_W):
            out_specs.append(tok(w))
            out_shape.append(jax.ShapeDtypeStruct((BATCH, n_tiles * TM, w), _BF16))
    outs = pl.pallas_call(
        kern,
        grid=(BATCH, n_tiles),
        in_specs=in_specs,
        out_specs=out_specs,
        out_shape=out_shape,
        compiler_params=pltpu.CompilerParams(
            dimension_semantics=("arbitrary", "arbitrary"), vmem_limit_bytes=VMEM_LIMIT),
        name=name,
    )(*args)
    return outs[0] if mix is not None else outs


def _head_norm(t, g, bd):
    width = t.shape[1]
    outs = []
    for k in range(0, width, MXU_TILE):
        w = min(MXU_TILE, width - k)
        tc = t[:, k:k + w]
        ms = jnp.dot((tc * tc).astype(_BF16), bd[:w, :w], preferred_element_type=_F32)
        outs.append(tc * lax.rsqrt(ms + NORM_EPS))
    out = outs[0] if len(outs) == 1 else jnp.concatenate(outs, axis=-1)
    return out * g


def _rope(t, cos, sin_signed, first_half):
    outs = []
    for k in range(0, t.shape[1], LANES):
        tc = t[:, k:k + LANES]
        partner = jnp.where(first_half,
                            pltpu.roll(tc, LANES - HEAD_DIM // 4, 1),
                            pltpu.roll(tc, HEAD_DIM // 4, 1))
        outs.append(tc * cos + partner * sin_signed)
    return outs[0] if len(outs) == 1 else jnp.concatenate(outs, axis=-1)


N_PROJ_IN = 6


def _project(x, mod, g_ref, win_ref, g4_ref, bd_ref, cos_ref, sin_ref,
             qa_ref, qb_ref, ka_ref, va_ref, kb_ref, vb_ref):
    h = _norm_mod(x, g_ref[...], mod[3:4], mod[4:5])
    px = jnp.dot(h.astype(_BF16), win_ref[0], preferred_element_type=_F32)
    bd = bd_ref[...]
    g4 = g4_ref[...]
    cos = cos_ref[...]
    sin = sin_ref[...]
    lane = lax.broadcasted_iota(jnp.int32, (1, LANES), 1)
    first_half = (lane % (HEAD_DIM // 2)) < (HEAD_DIM // 4)
    q_scale = HEAD_DIM ** -0.5 * LOG2E

    o1, o2, o3 = Q_W, Q_W + NA_W, Q_W + 2 * NA_W
    o4 = o3 + WA_KV_W
    qa = _head_norm(px[:, :NA_W], g4[0:1, :NA_W], bd)
    qa_ref[0] = (qa * q_scale).astype(_BF16)
    qb = _rope(_head_norm(px[:, NA_W:Q_W], g4[2:3, :WA_W], bd), cos, sin, first_half)
    qb_ref[0] = (qb * q_scale).astype(_BF16)
    ka = _head_norm(px[:, o1:o2], g4[1:2, :NA_W], bd)
    ka_ref[0] = ka.astype(_BF16)
    va_ref[0] = px[:, o2:o3].astype(_BF16)
    kb = _rope(_head_norm(px[:, o3:o4], g4[3:4, :WA_KV_W], bd), cos, sin, first_half)
    kb_ref[0] = kb.astype(_BF16)
    vb_ref[0] = px[:, o4:].astype(_BF16)


def _lane_masks():
    lane = lax.broadcasted_iota(jnp.int32, (1, LANES), 1)
    lo = lane < HEAD_DIM
    return lo, lo.astype(_BF16), (~lo).astype(_BF16)


def _softmax_pv(scores, values, extra_logit=None):
    m = _row_max(scores, extra_logit)
    return _normalise(_exp_pv(scores, m, values, extra_logit))


def _row_max(scores, extra_logit=None):
    m = jnp.max(_fold_lane_groups(scores, jnp.maximum), axis=-1, keepdims=True)
    return m if extra_logit is None else jnp.maximum(m, extra_logit)


def _exp_pv(scores, m, values, extra_logit=None):
    acc = None
    for s, v in zip(scores, values):
        v1 = jnp.concatenate([v, jnp.ones((v.shape[0], LANES), _BF16)], axis=1)
        pv = jnp.dot(jnp.exp2(s - m).astype(_BF16), v1, preferred_element_type=_F32)
        acc = pv if acc is None else acc + pv
    if extra_logit is not None:
        acc = jnp.concatenate([acc[:, :LANES], acc[:, LANES:] + jnp.exp2(extra_logit - m)], axis=1)
    return acc


def _normalise(acc):
    return acc[:, :LANES] / acc[:, LANES:]


def _fold_lane_groups(blocks, op):
    acc = None
    for s in blocks:
        for k in range(0, s.shape[1], LANES):
            c = s[:, k:k + LANES]
            acc = c if acc is None else op(acc, c)
    return acc


def _emit_staged(n_units, stages, lags):
    state = {}
    for step in range(n_units + lags[-1]):
        for stage, lag in zip(stages, lags):
            u = step - lag
            if 0 <= u < n_units:
                state[u] = stage(u, state.get(u))


def _na_kernel(q_ref, kp_ref, kc_ref, kn_ref, vp_ref, vc_ref, vn_ref, kx_ref, vx_ref,
               tbl_ref, o_ref, kbuf, vbuf):
    rb = pl.program_id(1)
    kbuf[0:NA_QT] = kp_ref[0]
    kbuf[NA_QT:2 * NA_QT] = kc_ref[0]
    kbuf[2 * NA_QT:3 * NA_QT] = kn_ref[0]
    vbuf[0:NA_QT] = vp_ref[0]
    vbuf[NA_QT:2 * NA_QT] = vc_ref[0]
    vbuf[2 * NA_QT:3 * NA_QT] = vn_ref[0]
    lo, m_lo, m_hi = _lane_masks()
    n_win = NA_KH * GRID_W
    n_pairs = NA_HEADS // 2

    def geometry(u):
        j, p = divmod(u, n_pairs)
        r = rb * NA_RB + j
        r0 = jnp.clip(r - NA_KH // 2, 0, ROWS - NA_KH)
        dr0 = r0 - r + (NA_KH - 1)
        off = pl.multiple_of((r0 - rb * NA_RB + NA_RB) * GRID_W, GRID_W)
        return j, p, dr0, off, slice(p * LANES, (p + 1) * LANES)

    def score_stage(u, _):
        j, _, _, off, cols = geometry(u)
        qp = q_ref[0, j * GRID_W:(j + 1) * GRID_W, cols]
        q2 = jnp.concatenate([qp * m_lo, qp * m_hi], axis=0)
        s_nb = lax.dot_general(q2, kbuf[pl.ds(off, n_win), cols], _NT,
                               preferred_element_type=_F32)
        s_cx = lax.dot_general(q2, kx_ref[0, :, cols], _NT, preferred_element_type=_F32)
        return s_nb, s_cx

    def max_stage(u, s):
        _, p, dr0, _, _ = geometry(u)
        s_nb, s_cx = s
        bias = jnp.concatenate([
            jnp.concatenate([tbl_ref[0, 2 * p + hh, pl.ds(dr0 + 2 * k, 1)][0]
                             for k in range(NA_KH // 2)], axis=1)
            for hh in range(2)], axis=0)
        sc = [s_nb + bias, s_cx]
        return sc, _row_max(sc)

    def pv_stage(u, s):
        _, _, _, off, cols = geometry(u)
        sc, m = s
        return _exp_pv(sc, m, [vbuf[pl.ds(off, n_win), cols], vx_ref[0, :, cols]])

    def out_stage(u, s):
        j, _, _, _, cols = geometry(u)
        o2 = _normalise(s)
        o_ref[0, j * GRID_W:(j + 1) * GRID_W, cols] = (
            jnp.where(lo, o2[:GRID_W], o2[GRID_W:]).astype(_BF16))

    _emit_staged(NA_RB * n_pairs, [score_stage, max_stage, pv_stage, out_stage], STAGE_LAGS)


def _na_attention(qa, ka, va, tbl, l):
    n_rb = ROWS // NA_RB
    blk = lambda f: pl.BlockSpec((1, NA_QT, NA_W), f)
    prev = lambda b, i: (b, jnp.maximum(i - 1, 0), 0)
    cur = lambda b, i: (b, i, 0)
    nxt = lambda b, i: (b, jnp.minimum(i + 1, n_rb - 1), 0)
    ctx = pl.BlockSpec((1, CTX_LEN, NA_W), lambda b, i: (b, SEQ // CTX_LEN, 0))
    return pl.pallas_call(
        _na_kernel,
        grid=(BATCH, n_rb),
        in_specs=[blk(cur), blk(prev), blk(cur), blk(nxt), blk(prev), blk(cur), blk(nxt),
                  ctx, ctx,
                  pl.BlockSpec((1,) + tbl.shape[1:], lambda b, i: (l, 0, 0, 0, 0))],
        out_specs=blk(cur),
        out_shape=jax.ShapeDtypeStruct((BATCH, SEQ, NA_W), _BF16),
        scratch_shapes=[pltpu.VMEM((3 * NA_QT, NA_W), _BF16),
                        pltpu.VMEM((3 * NA_QT, NA_W), _BF16)],
        compiler_params=pltpu.CompilerParams(
            dimension_semantics=("arbitrary", "arbitrary"), vmem_limit_bytes=VMEM_LIMIT),
        name="na_attention",
    )(qa, ka, ka, ka, va, va, va, ka, va, tbl)


def _stack_heads(q, m_lo, m_hi):
    parts = []
    for pos in range(WA_HEADS):
        qp = q[:, (pos // 2) * LANES:(pos // 2 + 1) * LANES]
        parts.append(qp * (m_lo if pos % 2 == 0 else m_hi))
    return jnp.concatenate(parts, axis=0)


def _unstack_heads(o, rows, lo):
    outs = []
    for g in range(WA_HEADS // 2):
        outs.append(jnp.where(lo, o[(2 * g) * rows:(2 * g + 1) * rows],
                              o[(2 * g + 1) * rows:(2 * g + 2) * rows]))
    return jnp.concatenate(outs, axis=1)


def _sink_column(sink_ref, rows):
    return jnp.concatenate(
        [jnp.full((rows, 1), sink_ref[pos], _F32) for pos in range(WA_HEADS)], axis=0)


def _wa_kernel(sink_ref, q_ref, kp_ref, kc_ref, kn_ref, vp_ref, vc_ref, vn_ref, kx_ref, vx_ref,
               o_ref, kbuf, vbuf):
    i = pl.program_id(1)
    kbuf[0:WA_BLOCK] = kp_ref[0]
    kbuf[WA_BLOCK:WA_BLOCK + WA_QT] = kc_ref[0]
    kbuf[WA_BLOCK + WA_QT:] = kn_ref[0]
    vbuf[0:WA_BLOCK] = vp_ref[0]
    vbuf[WA_BLOCK:WA_BLOCK + WA_QT] = vc_ref[0]
    vbuf[WA_BLOCK + WA_QT:] = vn_ref[0]
    lo, m_lo, m_hi = _lane_masks()
    n_loc = 3 * WA_BLOCK
    q_off = lax.broadcasted_iota(jnp.int32, (WA_BLOCK, n_loc), 0)
    k_off = lax.broadcasted_iota(jnp.int32, (WA_BLOCK, n_loc), 1) - WA_BLOCK
    band = jnp.abs(k_off - q_off) <= WA_WINDOW
    n_pairs = WA_HEADS // 2

    def geometry(u):
        t, g = divmod(u, n_pairs)
        return t, g, slice(t * WA_BLOCK, (t + 1) * WA_BLOCK), slice(g * LANES, (g + 1) * LANES)

    def sink_col(g):
        return jnp.concatenate(
            [jnp.full((WA_BLOCK, 1), sink_ref[2 * g + hh], _F32) for hh in range(2)], axis=0)

    def score_stage(u, _):
        t, _, rows, cols = geometry(u)
        qp = q_ref[0, rows, cols]
        q2 = jnp.concatenate([qp * m_lo, qp * m_hi], axis=0)
        s_loc = lax.dot_general(q2, kbuf[t * WA_BLOCK:t * WA_BLOCK + n_loc, :], _NT,
                                preferred_element_type=_F32)
        s_cx = lax.dot_general(q2, kx_ref[0], _NT, preferred_element_type=_F32)
        return s_loc, s_cx

    def max_stage(u, s):
        t, g, _, _ = geometry(u)
        s_loc, s_cx = s
        k_pos = (i * WA_SUB + t) * WA_BLOCK + k_off
        valid = band & (k_pos >= 0) & (k_pos < SEQ)
        s_loc = jnp.where(valid[None], s_loc.reshape(2, WA_BLOCK, n_loc), NEG_INF)
        sc = [s_loc.reshape(2 * WA_BLOCK, n_loc), s_cx]
        return sc, _row_max(sc, sink_col(g))

    def pv_stage(u, s):
        t, g, _, _ = geometry(u)
        sc, m = s
        return _exp_pv(sc, m, [vbuf[t * WA_BLOCK:t * WA_BLOCK + n_loc, :], vx_ref[0]],
                       sink_col(g))

    def out_stage(u, s):
        _, _, rows, cols = geometry(u)
        o2 = _normalise(s)
        o_ref[0, rows, cols] = jnp.where(lo, o2[:WA_BLOCK], o2[WA_BLOCK:]).astype(_BF16)

    _emit_staged(WA_SUB * n_pairs, [score_stage, max_stage, pv_stage, out_stage], STAGE_LAGS)


def _wa_attention(qb, kb, vb, sink):
    n_blk = SEQ // WA_BLOCK
    qspec = pl.BlockSpec((1, WA_QT, WA_W), lambda b, i: (b, i, 0))
    cur = pl.BlockSpec((1, WA_QT, WA_KV_W), lambda b, i: (b, i, 0))
    prev = pl.BlockSpec((1, WA_BLOCK, WA_KV_W),
                        lambda b, i: (b, jnp.maximum(i * WA_SUB - 1, 0), 0))
    nxt = pl.BlockSpec((1, WA_BLOCK, WA_KV_W),
                       lambda b, i: (b, jnp.minimum((i + 1) * WA_SUB, n_blk - 1), 0))
    ctx = pl.BlockSpec((1, CTX_LEN, WA_KV_W), lambda b, i: (b, SEQ // CTX_LEN, 0))
    return pl.pallas_call(
        _wa_kernel,
        grid=(BATCH, SEQ // WA_QT),
        in_specs=[pl.BlockSpec(memory_space=pltpu.SMEM),
                  qspec, prev, cur, nxt, prev, cur, nxt, ctx, ctx],
        out_specs=qspec,
        out_shape=jax.ShapeDtypeStruct((BATCH, SEQ, WA_W), _BF16),
        scratch_shapes=[pltpu.VMEM((WA_QT + 2 * WA_BLOCK, WA_KV_W), _BF16),
                        pltpu.VMEM((WA_QT + 2 * WA_BLOCK, WA_KV_W), _BF16)],
        compiler_params=pltpu.CompilerParams(
            dimension_semantics=("arbitrary", "arbitrary"), vmem_limit_bytes=VMEM_LIMIT),
        name="wa_attention",
    )(sink, qb, kb, kb, kb, vb, vb, vb, kb, vb)


def _ctx_kernel(sink_ref, qa_ref, ka_ref, va_ref, qb_ref, kb_ref, vb_ref, oa_ref, ob_ref):
    lo, m_lo, m_hi = _lane_masks()
    qa = qa_ref[0]
    outs = []
    for p in range(NA_HEADS // 2):
        cols = slice(p * LANES, (p + 1) * LANES)
        qp = qa[:, cols]
        q2 = jnp.concatenate([qp * m_lo, qp * m_hi], axis=0)
        s = lax.dot_general(q2, ka_ref[0, :, cols], _NT, preferred_element_type=_F32)
        o2 = _softmax_pv([s], [va_ref[0, :, cols]])
        outs.append(jnp.where(lo, o2[:CTX_LEN], o2[CTX_LEN:]))
    oa_ref[0] = jnp.concatenate(outs, axis=1).astype(_BF16)

    q8 = _stack_heads(qb_ref[0], m_lo, m_hi)
    s = lax.dot_general(q8, kb_ref[0], _NT, preferred_element_type=_F32)
    o8 = _softmax_pv([s], [vb_ref[0]], extra_logit=_sink_column(sink_ref, CTX_LEN))
    ob_ref[0] = _unstack_heads(o8, CTX_LEN, lo).astype(_BF16)


def _ctx_attention(qa, ka, va, qb, kb, vb, sink):
    ctx = lambda w: pl.BlockSpec((1, CTX_LEN, w), lambda b: (b, SEQ // CTX_LEN, 0))
    out = lambda w: pl.BlockSpec((1, CTX_LEN, w), lambda b: (b, 0, 0))
    return pl.pallas_call(
        _ctx_kernel,
        grid=(BATCH,),
        in_specs=[pl.BlockSpec(memory_space=pltpu.SMEM),
                  ctx(NA_W), ctx(NA_W), ctx(NA_W), ctx(WA_W), ctx(WA_KV_W), ctx(WA_KV_W)],
        out_specs=[out(NA_W), out(WA_W)],
        out_shape=[jax.ShapeDtypeStruct((BATCH, CTX_LEN, NA_W), _BF16),
                   jax.ShapeDtypeStruct((BATCH, CTX_LEN, WA_W), _BF16)],
        compiler_params=pltpu.CompilerParams(
            dimension_semantics=("arbitrary",), vmem_limit_bytes=VMEM_LIMIT),
        name="ctx_attention",
    )(sink, qa, ka, va, qb, kb, vb)


def _rope_tables():
    rot = HEAD_DIM // 2
    inv_freq = ROPE_BASE ** (-jnp.arange(0, rot, 2, dtype=_F32) / rot)
    t = jnp.arange(SEQ)
    row = (t // GRID_W).astype(_F32)
    col = (t % GRID_W).astype(_F32)
    ang = jnp.stack([row[:, None] * inv_freq, col[:, None] * inv_freq], axis=1)
    d = np.arange(LANES) % HEAD_DIM
    axis, half, freq = d // rot, (d % rot) // (rot // 2), d % (rot // 2)
    ang_l = ang[:, axis, freq]
    sign = jnp.asarray(np.where(half == 0, -1.0, 1.0), _F32)
    cos_t = jnp.concatenate([jnp.cos(ang_l), jnp.ones((CTX_LEN, LANES), _F32)], axis=0)
    sin_t = jnp.concatenate([jnp.sin(ang_l) * sign, jnp.zeros((CTX_LEN, LANES), _F32)], axis=0)
    return cos_t, sin_t


def _na_bias_tables(rpb):
    col = np.arange(GRID_W)
    c0 = np.clip(col - NA_KW // 2, 0, GRID_W - NA_KW)
    in_win = (col[None, :] >= c0[:, None]) & (col[None, :] < c0[:, None] + NA_KW)
    dc = np.clip(col[None, :] - col[:, None] + (NA_KW - 1), 0, 2 * NA_KW - 2)
    onehot = jnp.asarray(np.arange(2 * NA_KW - 1)[:, None] == dc.reshape(1, -1), _F32)
    full = jnp.dot(rpb.astype(_F32).reshape(-1, 2 * NA_KW - 1), onehot,
                   precision=lax.Precision.HIGHEST)
    full = full.reshape(DEPTH, NA_HEADS, 2 * NA_KH - 1, GRID_W, GRID_W)
    full = jnp.where(in_win, full * LOG2E, NEG_INF)
    return jnp.concatenate([full[:, :, :-1], full[:, :, 1:]], axis=-1)


def kernel(x, c, ctx, c_ctx, w_ada, b_ada, norm_g, w_ffn_up, w_ffn_down, w_in, w_out,
           qk_norm_g, na_rpb, wa_sink):
    group = WA_HEADS // WA_KV_HEADS
    w_up_b = w_ffn_up.astype(_BF16)
    w_dn_b = w_ffn_down.astype(_BF16)
    w_in_b = w_in.astype(_BF16)
    wq_b = (w_in_b[:, :, NA_W:Q_W].reshape(DEPTH, D_MODEL, WA_KV_HEADS, group, HEAD_DIM)
            .transpose(0, 1, 3, 2, 4).reshape(DEPTH, D_MODEL, WA_W))
    w_in_b = jnp.concatenate([w_in_b[:, :, :NA_W], wq_b, w_in_b[:, :, Q_W:]], axis=2)
    w_out_b = w_out.astype(_BF16)
    wo_b = (w_out_b[:, NA_W:].reshape(DEPTH, WA_KV_HEADS, group, HEAD_DIM, D_MODEL)
            .transpose(0, 2, 1, 3, 4).reshape(DEPTH, WA_W, D_MODEL))
    w_out_b = jnp.concatenate([w_out_b[:, :NA_W], wo_b], axis=1)
    sink_p = ((wa_sink.astype(_F32) * LOG2E).reshape(DEPTH, WA_KV_HEADS, group)
              .transpose(0, 2, 1).reshape(DEPTH, WA_HEADS))
    g4 = jnp.tile(qk_norm_g.astype(_F32), (1, 1, NA_W // HEAD_DIM))
    blk = np.arange(MXU_TILE) // HEAD_DIM
    bd = jnp.asarray((blk[:, None] == blk[None, :]) / HEAD_DIM, _BF16)
    cos_t, sin_t = _rope_tables()
    na_tbl = _na_bias_tables(na_rpb)

    cond = jnp.zeros((N_COND, D_MODEL), _F32).at[:BATCH].set(c).at[CTX_COND].set(c_ctx)
    mods = _ada_params(cond, w_ada, b_ada).reshape(DEPTH, N_COND, N_MOD, D_MODEL)

    x_all = (x, ctx)
    for l in range(DEPTH):
        last = l == DEPTH - 1
        mod_l = mods[l]
        x_all, qa, qb, ka, va, kb, vb = _ffn(
            x_all, mod_l, norm_g[l, 0:1], w_up_b, w_dn_b, l, 0, N_TILES,
            proj=(norm_g[l, 1:2], w_in_b, g4[l], bd, cos_t, sin_t))
        attn_a = _na_attention(qa, ka, va, na_tbl, l)
        attn_b = _wa_attention(qb, kb, vb, sink_p[l])
        if last:
            ctx_a = jnp.zeros((BATCH, CTX_LEN, NA_W), _BF16)
            ctx_b = jnp.zeros((BATCH, CTX_LEN, WA_W), _BF16)
        else:
            ctx_a, ctx_b = _ctx_attention(qa, ka, va, qb, kb, vb, sink_p[l])
        x_all = _ffn(x_all, mod_l, norm_g[l, 2:3], w_up_b, w_dn_b, l, 1,
                     CTX_TILE if last else N_TILES, mix=(attn_a, attn_b, ctx_a, ctx_b, w_out_b))
    return x_all
```

```python
import functools

import numpy as np
import jax
import jax.numpy as jnp
from jax import lax
from jax.experimental import pallas as pl
from jax.experimental.pallas import tpu as pltpu

D_MODEL = 1024
BATCH = 4
SEQ = 8192
DEPTH = 4
CTX_LEN = 256
GRID_W = 64
HEAD_DIM = 64
N_HEADS = D_MODEL // HEAD_DIM
NA_HEADS = N_HEADS // 2
WA_HEADS = N_HEADS - NA_HEADS
WA_KV_HEADS = max(1, WA_HEADS // 4)
NA_KH = 8
NA_KW = 16
WA_WINDOW = 128
WA_BLOCK = 128
D_FF = ((8 * D_MODEL // 3 + 127) // 128) * 128
ROPE_BASE = 10000.0
N_MOD = 9
MACARON_W = 0.5
NORM_EPS = 1e-6
NEG_INF = -1e30
LOG2E = 1.4426950408889634
NA_W = NA_HEADS * HEAD_DIM
WA_W = WA_HEADS * HEAD_DIM
WA_KV_W = WA_KV_HEADS * HEAD_DIM
Q_W = NA_W + WA_W
IN_W = Q_W + 2 * NA_W + 2 * WA_KV_W

ROWS = SEQ // GRID_W
TOK = SEQ + CTX_LEN
TM = 256
N_TILES = TOK // TM
CTX_TILE = SEQ // TM
N_COND = 8
CTX_COND = BATCH
NA_RB = 16
NA_QT = NA_RB * GRID_W
WA_SUB = 8
WA_QT = WA_SUB * WA_BLOCK
STAGE_LAGS = (0, 1, 2, 3)
LANES = 128
MXU_TILE = 256
VMEM_LIMIT = 56 * 1024 * 1024

_F32 = jnp.float32
_BF16 = jnp.bfloat16
_NT = (((1,), (1,)), ((), ()))


def _sigmoid(v):
    return 1.0 / (1.0 + jnp.exp(-v))


def _norm_mod(x, g, shift, scale):
    ms = jnp.mean(x * x, axis=-1, keepdims=True)
    h = x * lax.rsqrt(ms + NORM_EPS) * g
    return h * (1.0 + scale) + shift


def _cond_index(b, t):
    return jnp.where(t == CTX_TILE, CTX_COND, b)


ADA_TN = 1152


def _ada_kernel(cond_ref, w_ref, b_ref, o_ref):
    cond = cond_ref[...]
    a = (cond * _sigmoid(cond)).astype(_BF16)
    o_ref[0] = jnp.dot(a, w_ref[0].astype(_BF16), preferred_element_type=_F32) + b_ref[0]


def _ada_params(cond, w_ada, b_ada):
    n_out = N_MOD * D_MODEL
    return pl.pallas_call(
        _ada_kernel,
        grid=(DEPTH, n_out // ADA_TN),
        in_specs=[
            pl.BlockSpec((N_COND, D_MODEL), lambda l, n: (0, 0)),
            pl.BlockSpec((1, D_MODEL, ADA_TN), lambda l, n: (l, 0, n)),
            pl.BlockSpec((1, 1, ADA_TN), lambda l, n: (l, 0, n)),
        ],
        out_specs=pl.BlockSpec((1, N_COND, ADA_TN), lambda l, n: (l, 0, n)),
        out_shape=jax.ShapeDtypeStruct((DEPTH, N_COND, n_out), _F32),
        compiler_params=pltpu.CompilerParams(vmem_limit_bytes=VMEM_LIMIT),
        name="ada_params",
    )(cond, w_ada, b_ada.reshape(DEPTH, 1, n_out))


def _swiglu_half_step(x, mod, g, wup_ref, wdn_ref, j):
    h = _norm_mod(x, g, mod[j:j + 1], mod[j + 1:j + 2])
    u = jnp.dot(h.astype(_BF16), wup_ref[0, 0], preferred_element_type=_F32)
    gt = u[:, :D_FF]
    up = u[:, D_FF:]
    a = (gt * _sigmoid(gt) * up).astype(_BF16)
    y = jnp.dot(a, wdn_ref[0, 0], preferred_element_type=_F32)
    return x + (MACARON_W * mod[j + 2:j + 3]) * y


def _pre_mix(x, mod_ref, g_ref, wup_ref, wdn_ref, proj_refs, o_ref, qkv_refs):
    mod = mod_ref[0]
    x = _swiglu_half_step(x, mod, g_ref[...], wup_ref, wdn_ref, 0)
    o_ref[0] = x
    _project(x, mod, *proj_refs, *qkv_refs)


def _ffn_kernel(x_ref, mod_ref, g_ref, wup_ref, wdn_ref, *refs):
    _pre_mix(x_ref[0], mod_ref, g_ref, wup_ref, wdn_ref, refs[:N_PROJ_IN], refs[N_PROJ_IN],
             refs[N_PROJ_IN + 1:])


def _entry_ffn_kernel(xl_ref, xc_ref, mod_ref, g_ref, wup_ref, wdn_ref, *refs):
    x = jnp.where(pl.program_id(1) == CTX_TILE, xc_ref[0], xl_ref[0])
    _pre_mix(x, mod_ref, g_ref, wup_ref, wdn_ref, refs[:N_PROJ_IN], refs[N_PROJ_IN],
             refs[N_PROJ_IN + 1:])


def _mix_ffn_kernel(x_ref, mod_ref, g_ref, wup_ref, wdn_ref, al_ref, bl_ref, ac_ref, bc_ref,
                    wo_ref, o_ref):
    is_ctx = pl.program_id(1) == CTX_TILE
    aa = jnp.where(is_ctx, ac_ref[0], al_ref[0])
    ab = jnp.where(is_ctx, bc_ref[0], bl_ref[0])
    y = (jnp.dot(aa, wo_ref[0, :NA_W], preferred_element_type=_F32)
         + jnp.dot(ab, wo_ref[0, NA_W:], preferred_element_type=_F32))
    mod = mod_ref[0]
    x = x_ref[0] + mod[5:6] * y
    o_ref[0] = _swiglu_half_step(x, mod, g_ref[...], wup_ref, wdn_ref, 6)


def _ffn(x_in, mod_l, g, w_up, w_dn, l, half, n_tiles, proj=None, mix=None):
    const = pl.Buffered(1)
    tok = lambda w: pl.BlockSpec((1, TM, w), lambda b, t: (b, t, 0))
    lat = lambda w: pl.BlockSpec((1, TM, w), lambda b, t: (b, jnp.minimum(t, CTX_TILE - 1), 0))
    ctx = lambda w: pl.BlockSpec((1, CTX_LEN, w), lambda b, t: (b, 0, 0))
    entry = isinstance(x_in, tuple)
    assert (proj is None) != (mix is None) and not (entry and proj is None)
    in_specs = ([lat(D_MODEL), ctx(D_MODEL)] if entry else [tok(D_MODEL)]) + [
        pl.BlockSpec((1, N_MOD, D_MODEL), lambda b, t: (_cond_index(b, t), 0, 0)),
        pl.BlockSpec((1, D_MODEL), lambda b, t: (0, 0)),
        pl.BlockSpec((1, 1, D_MODEL, 2 * D_FF), lambda b, t: (l, half, 0, 0), pipeline_mode=const),
        pl.BlockSpec((1, 1, D_FF, D_MODEL), lambda b, t: (l, half, 0, 0), pipeline_mode=const),
    ]
    args = (list(x_in) if entry else [x_in]) + [mod_l, g, w_up, w_dn]
    out_specs = [tok(D_MODEL)]
    out_shape = [jax.ShapeDtypeStruct((BATCH, n_tiles * TM, D_MODEL), _F32)]
    if mix is not None:
        kern, name = _mix_ffn_kernel, "mix_ffn"
        in_specs += [lat(NA_W), lat(WA_W), ctx(NA_W), ctx(WA_W),
                     pl.BlockSpec((1, Q_W, D_MODEL), lambda b, t: (l, 0, 0), pipeline_mode=const)]
        args += list(mix)
    else:
        kern, name = (_entry_ffn_kernel, "entry_ffn") if entry else (_ffn_kernel, "ffn")
        in_specs += [
            pl.BlockSpec((1, D_MODEL), lambda b, t: (0, 0)),
            pl.BlockSpec((1, D_MODEL, IN_W), lambda b, t: (l, 0, 0), pipeline_mode=const),
            pl.BlockSpec((4, NA_W), lambda b, t: (0, 0)),
            pl.BlockSpec((MXU_TILE, MXU_TILE), lambda b, t: (0, 0)),
            pl.BlockSpec((TM, LANES), lambda b, t: (t, 0)),
            pl.BlockSpec((TM, LANES), lambda b, t: (t, 0)),
        ]
        assert len(proj) == N_PROJ_IN
        args += list(proj)
        for w in (NA_W, WA_W, NA_W, NA_W, WA_KV_W, WA_KV_W):
            out_specs.append(tok(w))
            out_shape.append(jax.ShapeDtypeStruct((BATCH, n_tiles * TM, w), _BF16))
    outs = pl.pallas_call(
        kern,
        grid=(BATCH, n_tiles),
        in_specs=in_specs,
        out_specs=out_specs,
        out_shape=out_shape,
        compiler_params=pltpu.CompilerParams(
            dimension_semantics=("arbitrary", "arbitrary"), vmem_limit_bytes=VMEM_LIMIT),
        name=name,
    )(*args)
    return outs[0] if mix is not None else outs


def _head_norm(t, g, bd):
    width = t.shape[1]
    outs = []
    for k in range(0, width, MXU_TILE):
        w = min(MXU_TILE, width - k)
        tc = t[:, k:k + w]
        ms = jnp.dot((tc * tc).astype(_BF16), bd[:w, :w], preferred_element_type=_F32)
        outs.append(tc * lax.rsqrt(ms + NORM_EPS))
    out = outs[0] if len(outs) == 1 else jnp.concatenate(outs, axis=-1)
    return out * g


def _rope(t, cos, sin_signed, first_half):
    outs = []
    for k in range(0, t.shape[1], LANES):
        tc = t[:, k:k + LANES]
        partner = jnp.where(first_half,
                            pltpu.roll(tc, LANES - HEAD_DIM // 4, 1),
                            pltpu.roll(tc, HEAD_DIM // 4, 1))
        outs.append(tc * cos + partner * sin_signed)
    return outs[0] if len(outs) == 1 else jnp.concatenate(outs, axis=-1)


N_PROJ_IN = 6


def _project(x, mod, g_ref, win_ref, g4_ref, bd_ref, cos_ref, sin_ref,
             qa_ref, qb_ref, ka_ref, va_ref, kb_ref, vb_ref):
    h = _norm_mod(x, g_ref[...], mod[3:4], mod[4:5])
    px = jnp.dot(h.astype(_BF16), win_ref[0], preferred_element_type=_F32)
    bd = bd_ref[...]
    g4 = g4_ref[...]
    cos = cos_ref[...]
    sin = sin_ref[...]
    lane = lax.broadcasted_iota(jnp.int32, (1, LANES), 1)
    first_half = (lane % (HEAD_DIM // 2)) < (HEAD_DIM // 4)
    q_scale = HEAD_DIM ** -0.5 * LOG2E

    o1, o2, o3 = Q_W, Q_W + NA_W, Q_W + 2 * NA_W
    o4 = o3 + WA_KV_W
    qa = _head_norm(px[:, :NA_W], g4[0:1, :NA_W], bd)
    qa_ref[0] = (qa * q_scale).astype(_BF16)
    qb = _rope(_head_norm(px[:, NA_W:Q_W], g4[2:3, :WA_W], bd), cos, sin, first_half)
    qb_ref[0] = (qb * q_scale).astype(_BF16)
    ka = _head_norm(px[:, o1:o2], g4[1:2, :NA_W], bd)
    ka_ref[0] = ka.astype(_BF16)
    va_ref[0] = px[:, o2:o3].astype(_BF16)
    kb = _rope(_head_norm(px[:, o3:o4], g4[3:4, :WA_KV_W], bd), cos, sin, first_half)
    kb_ref[0] = kb.astype(_BF16)
    vb_ref[0] = px[:, o4:].astype(_BF16)


def _lane_masks():
    lane = lax.broadcasted_iota(jnp.int32, (1, LANES), 1)
    lo = lane < HEAD_DIM
    return lo, lo.astype(_BF16), (~lo).astype(_BF16)


def _softmax_pv(scores, values, extra_logit=None):
    m = _row_max(scores, extra_logit)
    return _normalise(_exp_pv(scores, m, values, extra_logit))


def _row_max(scores, extra_logit=None):
    m = jnp.max(_fold_lane_groups(scores, jnp.maximum), axis=-1, keepdims=True)
    return m if extra_logit is None else jnp.maximum(m, extra_logit)


def _exp_pv(scores, m, values, extra_logit=None):
    acc = None
    for s, v in zip(scores, values):
        v1 = jnp.concatenate([v, jnp.ones((v.shape[0], LANES), _BF16)], axis=1)
        pv = jnp.dot(jnp.exp2(s - m).astype(_BF16), v1, preferred_element_type=_F32)
        acc = pv if acc is None else acc + pv
    if extra_logit is not None:
        acc = jnp.concatenate([acc[:, :LANES], acc[:, LANES:] + jnp.exp2(extra_logit - m)], axis=1)
    return acc


def _normalise(acc):
    return acc[:, :LANES] / acc[:, LANES:]


def _fold_lane_groups(blocks, op):
    acc = None
    for s in blocks:
        for k in range(0, s.shape[1], LANES):
            c = s[:, k:k + LANES]
            acc = c if acc is None else op(acc, c)
    return acc


def _emit_staged(n_units, stages, lags):
    state = {}
    for step in range(n_units + lags[-1]):
        for stage, lag in zip(stages, lags):
            u = step - lag
            if 0 <= u < n_units:
                state[u] = stage(u, state.get(u))


def _na_kernel(q_ref, kp_ref, kc_ref, kn_ref, vp_ref, vc_ref, vn_ref, kx_ref, vx_ref,
               tbl_ref, o_ref, kbuf, vbuf):
    rb = pl.program_id(1)
    kbuf[0:NA_QT] = kp_ref[0]
    kbuf[NA_QT:2 * NA_QT] = kc_ref[0]
    kbuf[2 * NA_QT:3 * NA_QT] = kn_ref[0]
    vbuf[0:NA_QT] = vp_ref[0]
    vbuf[NA_QT:2 * NA_QT] = vc_ref[0]
    vbuf[2 * NA_QT:3 * NA_QT] = vn_ref[0]
    lo, m_lo, m_hi = _lane_masks()
    n_win = NA_KH * GRID_W
    n_pairs = NA_HEADS // 2

    def geometry(u):
        j, p = divmod(u, n_pairs)
        r = rb * NA_RB + j
        r0 = jnp.clip(r - NA_KH // 2, 0, ROWS - NA_KH)
        dr0 = r0 - r + (NA_KH - 1)
        off = pl.multiple_of((r0 - rb * NA_RB + NA_RB) * GRID_W, GRID_W)
        return j, p, dr0, off, slice(p * LANES, (p + 1) * LANES)

    def score_stage(u, _):
        j, _, _, off, cols = geometry(u)
        qp = q_ref[0, j * GRID_W:(j + 1) * GRID_W, cols]
        q2 = jnp.concatenate([qp * m_lo, qp * m_hi], axis=0)
        s_nb = lax.dot_general(q2, kbuf[pl.ds(off, n_win), cols], _NT,
                               preferred_element_type=_F32)
        s_cx = lax.dot_general(q2, kx_ref[0, :, cols], _NT, preferred_element_type=_F32)
        return s_nb, s_cx

    def max_stage(u, s):
        _, p, dr0, _, _ = geometry(u)
        s_nb, s_cx = s
        bias = jnp.concatenate([
            jnp.concatenate([tbl_ref[0, 2 * p + hh, pl.ds(dr0 + 2 * k, 1)][0]
                             for k in range(NA_KH // 2)], axis=1)
            for hh in range(2)], axis=0)
        sc = [s_nb + bias, s_cx]
        return sc, _row_max(sc)

    def pv_stage(u, s):
        _, _, _, off, cols = geometry(u)
        sc, m = s
        return _exp_pv(sc, m, [vbuf[pl.ds(off, n_win), cols], vx_ref[0, :, cols]])

    def out_stage(u, s):
        j, _, _, _, cols = geometry(u)
        o2 = _normalise(s)
        o_ref[0, j * GRID_W:(j + 1) * GRID_W, cols] = (
            jnp.where(lo, o2[:GRID_W], o2[GRID_W:]).astype(_BF16))

    _emit_staged(NA_RB * n_pairs, [score_stage, max_stage, pv_stage, out_stage], STAGE_LAGS)


def _na_attention(qa, ka, va, tbl, l):
    n_rb = ROWS // NA_RB
    blk = lambda f: pl.BlockSpec((1, NA_QT, NA_W), f)
    prev = lambda b, i: (b, jnp.maximum(i - 1, 0), 0)
    cur = lambda b, i: (b, i, 0)
    nxt = lambda b, i: (b, jnp.minimum(i + 1, n_rb - 1), 0)
    ctx = pl.BlockSpec((1, CTX_LEN, NA_W), lambda b, i: (b, SEQ // CTX_LEN, 0))
    return pl.pallas_call(
        _na_kernel,
        grid=(BATCH, n_rb),
        in_specs=[blk(cur), blk(prev), blk(cur), blk(nxt), blk(prev), blk(cur), blk(nxt),
                  ctx, ctx,
                  pl.BlockSpec((1,) + tbl.shape[1:], lambda b, i: (l, 0, 0, 0, 0))],
        out_specs=blk(cur),
        out_shape=jax.ShapeDtypeStruct((BATCH, SEQ, NA_W), _BF16),
        scratch_shapes=[pltpu.VMEM((3 * NA_QT, NA_W), _BF16),
                        pltpu.VMEM((3 * NA_QT, NA_W), _BF16)],
        compiler_params=pltpu.CompilerParams(
            dimension_semantics=("arbitrary", "arbitrary"), vmem_limit_bytes=VMEM_LIMIT),
        name="na_attention",
    )(qa, ka, ka, ka, va, va, va, ka, va, tbl)


def _stack_heads(q, m_lo, m_hi):
    parts = []
    for pos in range(WA_HEADS):
        qp = q[:, (pos // 2) * LANES:(pos // 2 + 1) * LANES]
        parts.append(qp * (m_lo if pos % 2 == 0 else m_hi))
    return jnp.concatenate(parts, axis=0)


def _unstack_heads(o, rows, lo):
    outs = []
    for g in range(WA_HEADS // 2):
        outs.append(jnp.where(lo, o[(2 * g) * rows:(2 * g + 1) * rows],
                              o[(2 * g + 1) * rows:(2 * g + 2) * rows]))
    return jnp.concatenate(outs, axis=1)


def _sink_column(sink_ref, rows):
    return jnp.concatenate(
        [jnp.full((rows, 1), sink_ref[pos], _F32) for pos in range(WA_HEADS)], axis=0)


def _wa_kernel(sink_ref, q_ref, kp_ref, kc_ref, kn_ref, vp_ref, vc_ref, vn_ref, kx_ref, vx_ref,
               o_ref, kbuf, vbuf):
    i = pl.program_id(1)
    kbuf[0:WA_BLOCK] = kp_ref[0]
    kbuf[WA_BLOCK:WA_BLOCK + WA_QT] = kc_ref[0]
    kbuf[WA_BLOCK + WA_QT:] = kn_ref[0]
    vbuf[0:WA_BLOCK] = vp_ref[0]
    vbuf[WA_BLOCK:WA_BLOCK + WA_QT] = vc_ref[0]
    vbuf[WA_BLOCK + WA_QT:] = vn_ref[0]
    lo, m_lo, m_hi = _lane_masks()
    n_loc = 3 * WA_BLOCK
    q_off = lax.broadcasted_iota(jnp.int32, (WA_BLOCK, n_loc), 0)
    k_off = lax.broadcasted_iota(jnp.int32, (WA_BLOCK, n_loc), 1) - WA_BLOCK
    band = jnp.abs(k_off - q_off) <= WA_WINDOW
    n_pairs = WA_HEADS // 2

    def geometry(u):
        t, g = divmod(u, n_pairs)
        return t, g, slice(t * WA_BLOCK, (t + 1) * WA_BLOCK), slice(g * LANES, (g + 1) * LANES)

    def sink_col(g):
        return jnp.concatenate(
            [jnp.full((WA_BLOCK, 1), sink_ref[2 * g + hh], _F32) for hh in range(2)], axis=0)

    def score_stage(u, _):
        t, _, rows, cols = geometry(u)
        qp = q_ref[0, rows, cols]
        q2 = jnp.concatenate([qp * m_lo, qp * m_hi], axis=0)
        s_loc = lax.dot_general(q2, kbuf[t * WA_BLOCK:t * WA_BLOCK + n_loc, :], _NT,
                                preferred_element_type=_F32)
        s_cx = lax.dot_general(q2, kx_ref[0], _NT, preferred_element_type=_F32)
        return s_loc, s_cx

    def max_stage(u, s):
        t, g, _, _ = geometry(u)
        s_loc, s_cx = s
        k_pos = (i * WA_SUB + t) * WA_BLOCK + k_off
        valid = band & (k_pos >= 0) & (k_pos < SEQ)
        s_loc = jnp.where(valid[None], s_loc.reshape(2, WA_BLOCK, n_loc), NEG_INF)
        sc = [s_loc.reshape(2 * WA_BLOCK, n_loc), s_cx]
        return sc, _row_max(sc, sink_col(g))

    def pv_stage(u, s):
        t, g, _, _ = geometry(u)
        sc, m = s
        return _exp_pv(sc, m, [vbuf[t * WA_BLOCK:t * WA_BLOCK + n_loc, :], vx_ref[0]],
                       sink_col(g))

    def out_stage(u, s):
        _, _, rows, cols = geometry(u)
        o2 = _normalise(s)
        o_ref[0, rows, cols] = jnp.where(lo, o2[:WA_BLOCK], o2[WA_BLOCK:]).astype(_BF16)

    _emit_staged(WA_SUB * n_pairs, [score_stage, max_stage, pv_stage, out_stage], STAGE_LAGS)


def _wa_attention(qb, kb, vb, sink):
    n_blk = SEQ // WA_BLOCK
    qspec = pl.BlockSpec((1, WA_QT, WA_W), lambda b, i: (b, i, 0))
    cur = pl.BlockSpec((1, WA_QT, WA_KV_W), lambda b, i: (b, i, 0))
    prev = pl.BlockSpec((1, WA_BLOCK, WA_KV_W),
                        lambda b, i: (b, jnp.maximum(i * WA_SUB - 1, 0), 0))
    nxt = pl.BlockSpec((1, WA_BLOCK, WA_KV_W),
                       lambda b, i: (b, jnp.minimum((i + 1) * WA_SUB, n_blk - 1), 0))
    ctx = pl.BlockSpec((1, CTX_LEN, WA_KV_W), lambda b, i: (b, SEQ // CTX_LEN, 0))
    return pl.pallas_call(
        _wa_kernel,
        grid=(BATCH, SEQ // WA_QT),
        in_specs=[pl.BlockSpec(memory_space=pltpu.SMEM),
                  qspec, prev, cur, nxt, prev, cur, nxt, ctx, ctx],
        out_specs=qspec,
        out_shape=jax.ShapeDtypeStruct((BATCH, SEQ, WA_W), _BF16),
        scratch_shapes=[pltpu.VMEM((WA_QT + 2 * WA_BLOCK, WA_KV_W), _BF16),
                        pltpu.VMEM((WA_QT + 2 * WA_BLOCK, WA_KV_W), _BF16)],
        compiler_params=pltpu.CompilerParams(
            dimension_semantics=("arbitrary", "arbitrary"), vmem_limit_bytes=VMEM_LIMIT),
        name="wa_attention",
    )(sink, qb, kb, kb, kb, vb, vb, vb, kb, vb)


def _ctx_kernel(sink_ref, qa_ref, ka_ref, va_ref, qb_ref, kb_ref, vb_ref, oa_ref, ob_ref):
    lo, m_lo, m_hi = _lane_masks()
    qa = qa_ref[0]
    outs = []
    for p in range(NA_HEADS // 2):
        cols = slice(p * LANES, (p + 1) * LANES)
        qp = qa[:, cols]
        q2 = jnp.concatenate([qp * m_lo, qp * m_hi], axis=0)
        s = lax.dot_general(q2, ka_ref[0, :, cols], _NT, preferred_element_type=_F32)
        o2 = _softmax_pv([s], [va_ref[0, :, cols]])
        outs.append(jnp.where(lo, o2[:CTX_LEN], o2[CTX_LEN:]))
    oa_ref[0] = jnp.concatenate(outs, axis=1).astype(_BF16)

    q8 = _stack_heads(qb_ref[0], m_lo, m_hi)
    s = lax.dot_general(q8, kb_ref[0], _NT, preferred_element_type=_F32)
    o8 = _softmax_pv([s], [vb_ref[0]], extra_logit=_sink_column(sink_ref, CTX_LEN))
    ob_ref[0] = _unstack_heads(o8, CTX_LEN, lo).astype(_BF16)


def _ctx_attention(qa, ka, va, qb, kb, vb, sink):
    ctx = lambda w: pl.BlockSpec((1, CTX_LEN, w), lambda b: (b, SEQ // CTX_LEN, 0))
    out = lambda w: pl.BlockSpec((1, CTX_LEN, w), lambda b: (b, 0, 0))
    return pl.pallas_call(
        _ctx_kernel,
        grid=(BATCH,),
        in_specs=[pl.BlockSpec(memory_space=pltpu.SMEM),
                  ctx(NA_W), ctx(NA_W), ctx(NA_W), ctx(WA_W), ctx(WA_KV_W), ctx(WA_KV_W)],
        out_specs=[out(NA_W), out(WA_W)],
        out_shape=[jax.ShapeDtypeStruct((BATCH, CTX_LEN, NA_W), _BF16),
                   jax.ShapeDtypeStruct((BATCH, CTX_LEN, WA_W), _BF16)],
        compiler_params=pltpu.CompilerParams(
            dimension_semantics=("arbitrary",), vmem_limit_bytes=VMEM_LIMIT),
        name="ctx_attention",
    )(sink, qa, ka, va, qb, kb, vb)


def _rope_tables():
    rot = HEAD_DIM // 2
    inv_freq = ROPE_BASE ** (-jnp.arange(0, rot, 2, dtype=_F32) / rot)
    t = jnp.arange(SEQ)
    row = (t // GRID_W).astype(_F32)
    col = (t % GRID_W).astype(_F32)
    ang = jnp.stack([row[:, None] * inv_freq, col[:, None] * inv_freq], axis=1)
    d = np.arange(LANES) % HEAD_DIM
    axis, half, freq = d // rot, (d % rot) // (rot // 2), d % (rot // 2)
    ang_l = ang[:, axis, freq]
    sign = jnp.asarray(np.where(half == 0, -1.0, 1.0), _F32)
    cos_t = jnp.concatenate([jnp.cos(ang_l), jnp.ones((CTX_LEN, LANES), _F32)], axis=0)
    sin_t = jnp.concatenate([jnp.sin(ang_l) * sign, jnp.zeros((CTX_LEN, LANES), _F32)], axis=0)
    return cos_t, sin_t


def _na_bias_tables(rpb):
    col = np.arange(GRID_W)
    c0 = np.clip(col - NA_KW // 2, 0, GRID_W - NA_KW)
    in_win = (col[None, :] >= c0[:, None]) & (col[None, :] < c0[:, None] + NA_KW)
    dc = np.clip(col[None, :] - col[:, None] + (NA_KW - 1), 0, 2 * NA_KW - 2)
    onehot = jnp.asarray(np.arange(2 * NA_KW - 1)[:, None] == dc.reshape(1, -1), _F32)
    full = jnp.dot(rpb.astype(_F32).reshape(-1, 2 * NA_KW - 1), onehot,
                   precision=lax.Precision.HIGHEST)
    full = full.reshape(DEPTH, NA_HEADS, 2 * NA_KH - 1, GRID_W, GRID_W)
    full = jnp.where(in_win, full * LOG2E, NEG_INF)
    return jnp.concatenate([full[:, :, :-1], full[:, :, 1:]], axis=-1)


def kernel(x, c, ctx, c_ctx, w_ada, b_ada, norm_g, w_ffn_up, w_ffn_down, w_in, w_out,
           qk_norm_g, na_rpb, wa_sink):
    group = WA_HEADS // WA_KV_HEADS
    w_up_b = w_ffn_up.astype(_BF16)
    w_dn_b = w_ffn_down.astype(_BF16)
    w_in_b = w_in.astype(_BF16)
    wq_b = (w_in_b[:, :, NA_W:Q_W].reshape(DEPTH, D_MODEL, WA_KV_HEADS, group, HEAD_DIM)
            .transpose(0, 1, 3, 2, 4).reshape(DEPTH, D_MODEL, WA_W))
    w_in_b = jnp.concatenate([w_in_b[:, :, :NA_W], wq_b, w_in_b[:, :, Q_W:]], axis=2)
    w_out_b = w_out.astype(_BF16)
    wo_b = (w_out_b[:, NA_W:].reshape(DEPTH, WA_KV_HEADS, group, HEAD_DIM, D_MODEL)
            .transpose(0, 2, 1, 3, 4).reshape(DEPTH, WA_W, D_MODEL))
    w_out_b = jnp.concatenate([w_out_b[:, :NA_W], wo_b], axis=1)
    sink_p = ((wa_sink.astype(_F32) * LOG2E).reshape(DEPTH, WA_KV_HEADS, group)
              .transpose(0, 2, 1).reshape(DEPTH, WA_HEADS))
    g4 = jnp.tile(qk_norm_g.astype(_F32), (1, 1, NA_W // HEAD_DIM))
    blk = np.arange(MXU_TILE) // HEAD_DIM
    bd = jnp.asarray((blk[:, None] == blk[None, :]) / HEAD_DIM, _BF16)
    cos_t, sin_t = _rope_tables()
    na_tbl = _na_bias_tables(na_rpb)

    cond = jnp.zeros((N_COND, D_MODEL), _F32).at[:BATCH].set(c).at[CTX_COND].set(c_ctx)
    mods = _ada_params(cond, w_ada, b_ada).reshape(DEPTH, N_COND, N_MOD, D_MODEL)

    x_all = (x, ctx)
    for l in range(DEPTH):
        last = l == DEPTH - 1
        mod_l = mods[l]
        x_all, qa, qb, ka, va, kb, vb = _ffn(
            x_all, mod_l, norm_g[l, 0:1], w_up_b, w_dn_b, l, 0, N_TILES,
            proj=(norm_g[l, 1:2], w_in_b, g4[l], bd, cos_t, sin_t))
        attn_a = _na_attention(qa, ka, va, na_tbl, l)
        attn_b = _wa_attention(qb, kb, vb, sink_p[l])
        if last:
            ctx_a = jnp.zeros((BATCH, CTX_LEN, NA_W), _BF16)
            ctx_b = jnp.zeros((BATCH, CTX_LEN, WA_W), _BF16)
        else:
            ctx_a, ctx_b = _ctx_attention(qa, ka, va, qb, kb, vb, sink_p[l])
        x_all = _ffn(x_all, mod_l, norm_g[l, 2:3], w_up_b, w_dn_b, l, 1,
                     CTX_TILE if last else N_TILES, mix=(attn_a, attn_b, ctx_a, ctx_b, w_out_b))
    return x_all
```

```python
import functools

import numpy as np
import jax
import jax.numpy as jnp
from jax import lax
from jax.experimental import pallas as pl
from jax.experimental.pallas import tpu as pltpu

D_MODEL = 1024
BATCH = 4
SEQ = 8192
DEPTH = 4
CTX_LEN = 256
GRID_W = 64
HEAD_DIM = 64
N_HEADS = D_MODEL // HEAD_DIM
NA_HEADS = N_HEADS // 2
WA_HEADS = N_HEADS - NA_HEADS
WA_KV_HEADS = max(1, WA_HEADS // 4)
NA_KH = 8
NA_KW = 16
WA_WINDOW = 128
WA_BLOCK = 128
D_FF = ((8 * D_MODEL // 3 + 127) // 128) * 128
ROPE_BASE = 10000.0
N_MOD = 9
MACARON_W = 0.5
NORM_EPS = 1e-6
NEG_INF = -1e30
LOG2E = 1.4426950408889634
NA_W = NA_HEADS * HEAD_DIM
WA_W = WA_HEADS * HEAD_DIM
WA_KV_W = WA_KV_HEADS * HEAD_DIM
Q_W = NA_W + WA_W
IN_W = Q_W + 2 * NA_W + 2 * WA_KV_W

ROWS = SEQ // GRID_W
TOK = SEQ + CTX_LEN
TM = 256
N_TILES = TOK // TM
CTX_TILE = SEQ // TM
N_COND = 8
CTX_COND = BATCH
NA_RB = 32
NA_QT = NA_RB * GRID_W
WA_SUB = 8
WA_QT = WA_SUB * WA_BLOCK
STAGE_LAGS = (0, 1, 2, 3)
LANES = 128
MXU_TILE = 256
VMEM_LIMIT = 56 * 1024 * 1024

_F32 = jnp.float32
_BF16 = jnp.bfloat16
_NT = (((1,), (1,)), ((), ()))


def _sigmoid(v):
    return 1.0 / (1.0 + jnp.exp(-v))


def _norm_mod(x, g, shift, scale):
    ms = jnp.mean(x * x, axis=-1, keepdims=True)
    h = x * lax.rsqrt(ms + NORM_EPS) * g
    return h * (1.0 + scale) + shift


def _cond_index(b, t):
    return jnp.where(t == CTX_TILE, CTX_COND, b)


ADA_TN = 1152


def _ada_kernel(cond_ref, w_ref, b_ref, o_ref):
    cond = cond_ref[...]
    a = (cond * _sigmoid(cond)).astype(_BF16)
    o_ref[0] = jnp.dot(a, w_ref[0].astype(_BF16), preferred_element_type=_F32) + b_ref[0]


def _ada_params(cond, w_ada, b_ada):
    n_out = N_MOD * D_MODEL
    return pl.pallas_call(
        _ada_kernel,
        grid=(DEPTH, n_out // ADA_TN),
        in_specs=[
            pl.BlockSpec((N_COND, D_MODEL), lambda l, n: (0, 0)),
            pl.BlockSpec((1, D_MODEL, ADA_TN), lambda l, n: (l, 0, n)),
            pl.BlockSpec((1, 1, ADA_TN), lambda l, n: (l, 0, n)),
        ],
        out_specs=pl.BlockSpec((1, N_COND, ADA_TN), lambda l, n: (l, 0, n)),
        out_shape=jax.ShapeDtypeStruct((DEPTH, N_COND, n_out), _F32),
        compiler_params=pltpu.CompilerParams(vmem_limit_bytes=VMEM_LIMIT),
        name="ada_params",
    )(cond, w_ada, b_ada.reshape(DEPTH, 1, n_out))


def _swiglu_half_step(x, mod, g, wup_ref, wdn_ref, j):
    h = _norm_mod(x, g, mod[j:j + 1], mod[j + 1:j + 2])
    u = jnp.dot(h.astype(_BF16), wup_ref[0, 0], preferred_element_type=_F32)
    gt = u[:, :D_FF]
    up = u[:, D_FF:]
    a = (gt * _sigmoid(gt) * up).astype(_BF16)
    y = jnp.dot(a, wdn_ref[0, 0], preferred_element_type=_F32)
    return x + (MACARON_W * mod[j + 2:j + 3]) * y


def _pre_mix(x, mod_ref, g_ref, wup_ref, wdn_ref, proj_refs, o_ref, qkv_refs):
    mod = mod_ref[0]
    x = _swiglu_half_step(x, mod, g_ref[...], wup_ref, wdn_ref, 0)
    o_ref[0] = x
    _project(x, mod, *proj_refs, *qkv_refs)


def _ffn_kernel(x_ref, mod_ref, g_ref, wup_ref, wdn_ref, *refs):
    _pre_mix(x_ref[0], mod_ref, g_ref, wup_ref, wdn_ref, refs[:N_PROJ_IN], refs[N_PROJ_IN],
             refs[N_PROJ_IN + 1:])


def _entry_ffn_kernel(xl_ref, xc_ref, mod_ref, g_ref, wup_ref, wdn_ref, *refs):
    x = jnp.where(pl.program_id(1) == CTX_TILE, xc_ref[0], xl_ref[0])
    _pre_mix(x, mod_ref, g_ref, wup_ref, wdn_ref, refs[:N_PROJ_IN], refs[N_PROJ_IN],
             refs[N_PROJ_IN + 1:])


def _mix_ffn_kernel(x_ref, mod_ref, g_ref, wup_ref, wdn_ref, al_ref, bl_ref, ac_ref, bc_ref,
                    wo_ref, o_ref):
    is_ctx = pl.program_id(1) == CTX_TILE
    aa = jnp.where(is_ctx, ac_ref[0], al_ref[0])
    ab = jnp.where(is_ctx, bc_ref[0], bl_ref[0])
    y = (jnp.dot(aa, wo_ref[0, :NA_W], preferred_element_type=_F32)
         + jnp.dot(ab, wo_ref[0, NA_W:], preferred_element_type=_F32))
    mod = mod_ref[0]
    x = x_ref[0] + mod[5:6] * y
    o_ref[0] = _swiglu_half_step(x, mod, g_ref[...], wup_ref, wdn_ref, 6)


def _ffn(x_in, mod_l, g, w_up, w_dn, l, half, n_tiles, proj=None, mix=None):
    const = pl.Buffered(1)
    tok = lambda w: pl.BlockSpec((1, TM, w), lambda b, t: (b, t, 0))
    lat = lambda w: pl.BlockSpec((1, TM, w), lambda b, t: (b, jnp.minimum(t, CTX_TILE - 1), 0))
    ctx = lambda w: pl.BlockSpec((1, CTX_LEN, w), lambda b, t: (b, 0, 0))
    entry = isinstance(x_in, tuple)
    assert (proj is None) != (mix is None) and not (entry and proj is None)
    in_specs = ([lat(D_MODEL), ctx(D_MODEL)] if entry else [tok(D_MODEL)]) + [
        pl.BlockSpec((1, N_MOD, D_MODEL), lambda b, t: (_cond_index(b, t), 0, 0)),
        pl.BlockSpec((1, D_MODEL), lambda b, t: (0, 0)),
        pl.BlockSpec((1, 1, D_MODEL, 2 * D_FF), lambda b, t: (l, half, 0, 0), pipeline_mode=const),
        pl.BlockSpec((1, 1, D_FF, D_MODEL), lambda b, t: (l, half, 0, 0), pipeline_mode=const),
    ]
    args = (list(x_in) if entry else [x_in]) + [mod_l, g, w_up, w_dn]
    out_specs = [tok(D_MODEL)]
    out_shape = [jax.ShapeDtypeStruct((BATCH, n_tiles * TM, D_MODEL), _F32)]
    if mix is not None:
        kern, name = _mix_ffn_kernel, "mix_ffn"
        in_specs += [lat(NA_W), lat(WA_W), ctx(NA_W), ctx(WA_W),
                     pl.BlockSpec((1, Q_W, D_MODEL), lambda b, t: (l, 0, 0), pipeline_mode=const)]
        args += list(mix)
    else:
        kern, name = (_entry_ffn_kernel, "entry_ffn") if entry else (_ffn_kernel, "ffn")
        in_specs += [
            pl.BlockSpec((1, D_MODEL), lambda b, t: (0, 0)),
            pl.BlockSpec((1, D_MODEL, IN_W), lambda b, t: (l, 0, 0), pipeline_mode=const),
            pl.BlockSpec((4, NA_W), lambda b, t: (0, 0)),
            pl.BlockSpec((MXU_TILE, MXU_TILE), lambda b, t: (0, 0)),
            pl.BlockSpec((TM, LANES), lambda b, t: (t, 0)),
            pl.BlockSpec((TM, LANES), lambda b, t: (t, 0)),
        ]
        assert len(proj) == N_PROJ_IN
        args += list(proj)
        for w in (NA_W, WA_W, NA_W, NA_W, WA_KV_W, WA_KV_W):
            out_specs.append(tok(w))
            out_shape.append(jax.ShapeDtypeStruct((BATCH, n_tiles * TM, w), _BF16))
    outs = pl.pallas_call(
        kern,
        grid=(BATCH, n_tiles),
        in_specs=in_specs,
        out_specs=out_specs,
        out_shape=out_shape,
        compiler_params=pltpu.CompilerParams(
            dimension_semantics=("arbitrary", "arbitrary"), vmem_limit_bytes=VMEM_LIMIT),
        name=name,
    )(*args)
    return outs[0] if mix is not None else outs


def _head_norm(t, g, bd):
    width = t.shape[1]
    outs = []
    for k in range(0, width, MXU_TILE):
        w = min(MXU_TILE, width - k)
        tc = t[:, k:k + w]
        ms = jnp.dot((tc * tc).astype(_BF16), bd[:w, :w], preferred_element_type=_F32)
        outs.append(tc * lax.rsqrt(ms + NORM_EPS))
    out = outs[0] if len(outs) == 1 else jnp.concatenate(outs, axis=-1)
    return out * g


def _rope(t, cos, sin_signed, first_half):
    outs = []
    for k in range(0, t.shape[1], LANES):
        tc = t[:, k:k + LANES]
        partner = jnp.where(first_half,
                            pltpu.roll(tc, LANES - HEAD_DIM // 4, 1),
                            pltpu.roll(tc, HEAD_DIM // 4, 1))
        outs.append(tc * cos + partner * sin_signed)
    return outs[0] if len(outs) == 1 else jnp.concatenate(outs, axis=-1)


N_PROJ_IN = 6


def _project(x, mod, g_ref, win_ref, g4_ref, bd_ref, cos_ref, sin_ref,
             qa_ref, qb_ref, ka_ref, va_ref, kb_ref, vb_ref):
    h = _norm_mod(x, g_ref[...], mod[3:4], mod[4:5])
    px = jnp.dot(h.astype(_BF16), win_ref[0], preferred_element_type=_F32)
    bd = bd_ref[...]
    g4 = g4_ref[...]
    cos = cos_ref[...]
    sin = sin_ref[...]
    lane = lax.broadcasted_iota(jnp.int32, (1, LANES), 1)
    first_half = (lane % (HEAD_DIM // 2)) < (HEAD_DIM // 4)
    q_scale = HEAD_DIM ** -0.5 * LOG2E

    o1, o2, o3 = Q_W, Q_W + NA_W, Q_W + 2 * NA_W
    o4 = o3 + WA_KV_W
    qa = _head_norm(px[:, :NA_W], g4[0:1, :NA_W], bd)
    qa_ref[0] = (qa * q_scale).astype(_BF16)
    qb = _rope(_head_norm(px[:, NA_W:Q_W], g4[2:3, :WA_W], bd), cos, sin, first_half)
    qb_ref[0] = (qb * q_scale).astype(_BF16)
    ka = _head_norm(px[:, o1:o2], g4[1:2, :NA_W], bd)
    ka_ref[0] = ka.astype(_BF16)
    va_ref[0] = px[:, o2:o3].astype(_BF16)
    kb = _rope(_head_norm(px[:, o3:o4], g4[3:4, :WA_KV_W], bd), cos, sin, first_half)
    kb_ref[0] = kb.astype(_BF16)
    vb_ref[0] = px[:, o4:].astype(_BF16)


def _lane_masks():
    lane = lax.broadcasted_iota(jnp.int32, (1, LANES), 1)
    lo = lane < HEAD_DIM
    return lo, lo.astype(_BF16), (~lo).astype(_BF16)


def _softmax_pv(scores, values, extra_logit=None):
    m = _row_max(scores, extra_logit)
    return _normalise(_exp_pv(scores, m, values, extra_logit))


def _row_max(scores, extra_logit=None):
    m = jnp.max(_fold_lane_groups(scores, jnp.maximum), axis=-1, keepdims=True)
    return m if extra_logit is None else jnp.maximum(m, extra_logit)


def _exp_pv(scores, m, values, extra_logit=None):
    acc = None
    for s, v in zip(scores, values):
        v1 = jnp.concatenate([v, jnp.ones((v.shape[0], LANES), _BF16)], axis=1)
        pv = jnp.dot(jnp.exp2(s - m).astype(_BF16), v1, preferred_element_type=_F32)
        acc = pv if acc is None else acc + pv
    if extra_logit is not None:
        acc = jnp.concatenate([acc[:, :LANES], acc[:, LANES:] + jnp.exp2(extra_logit - m)], axis=1)
    return acc


def _normalise(acc):
    return acc[:, :LANES] / acc[:, LANES:]


def _fold_lane_groups(blocks, op):
    acc = None
    for s in blocks:
        for k in range(0, s.shape[1], LANES):
            c = s[:, k:k + LANES]
            acc = c if acc is None else op(acc, c)
    return acc


def _emit_staged(n_units, stages, lags):
    state = {}
    for step in range(n_units + lags[-1]):
        for stage, lag in zip(stages, lags):
            u = step - lag
            if 0 <= u < n_units:
                state[u] = stage(u, state.get(u))


def _na_kernel(q_ref, kp_ref, kc_ref, kn_ref, vp_ref, vc_ref, vn_ref, kx_ref, vx_ref,
               tbl_ref, o_ref, kbuf, vbuf):
    rb = pl.program_id(1)
    kbuf[0:NA_QT] = kp_ref[0]
    kbuf[NA_QT:2 * NA_QT] = kc_ref[0]
    kbuf[2 * NA_QT:3 * NA_QT] = kn_ref[0]
    vbuf[0:NA_QT] = vp_ref[0]
    vbuf[NA_QT:2 * NA_QT] = vc_ref[0]
    vbuf[2 * NA_QT:3 * NA_QT] = vn_ref[0]
    lo, m_lo, m_hi = _lane_masks()
    n_win = NA_KH * GRID_W
    n_pairs = NA_HEADS // 2

    def geometry(u):
        j, p = divmod(u, n_pairs)
        r = rb * NA_RB + j
        r0 = jnp.clip(r - NA_KH // 2, 0, ROWS - NA_KH)
        dr0 = r0 - r + (NA_KH - 1)
        off = pl.multiple_of((r0 - rb * NA_RB + NA_RB) * GRID_W, GRID_W)
        return j, p, dr0, off, slice(p * LANES, (p + 1) * LANES)

    def score_stage(u, _):
        j, _, _, off, cols = geometry(u)
        qp = q_ref[0, j * GRID_W:(j + 1) * GRID_W, cols]
        q2 = jnp.concatenate([qp * m_lo, qp * m_hi], axis=0)
        s_nb = lax.dot_general(q2, kbuf[pl.ds(off, n_win), cols], _NT,
                               preferred_element_type=_F32)
        s_cx = lax.dot_general(q2, kx_ref[0, :, cols], _NT, preferred_element_type=_F32)
        return s_nb, s_cx

    def max_stage(u, s):
        _, p, dr0, _, _ = geometry(u)
        s_nb, s_cx = s
        bias = jnp.concatenate([
            jnp.concatenate([tbl_ref[0, 2 * p + hh, pl.ds(dr0 + 2 * k, 1)][0]
                             for k in range(NA_KH // 2)], axis=1)
            for hh in range(2)], axis=0)
        sc = [s_nb + bias, s_cx]
        return sc, _row_max(sc)

    def pv_stage(u, s):
        _, _, _, off, cols = geometry(u)
        sc, m = s
        return _exp_pv(sc, m, [vbuf[pl.ds(off, n_win), cols], vx_ref[0, :, cols]])

    def out_stage(u, s):
        j, _, _, _, cols = geometry(u)
        o2 = _normalise(s)
        o_ref[0, j * GRID_W:(j + 1) * GRID_W, cols] = (
            jnp.where(lo, o2[:GRID_W], o2[GRID_W:]).astype(_BF16))

    _emit_staged(NA_RB * n_pairs, [score_stage, max_stage, pv_stage, out_stage], STAGE_LAGS)


def _na_attention(qa, ka, va, tbl, l):
    n_rb = ROWS // NA_RB
    blk = lambda f: pl.BlockSpec((1, NA_QT, NA_W), f)
    prev = lambda b, i: (b, jnp.maximum(i - 1, 0), 0)
    cur = lambda b, i: (b, i, 0)
    nxt = lambda b, i: (b, jnp.minimum(i + 1, n_rb - 1), 0)
    ctx = pl.BlockSpec((1, CTX_LEN, NA_W), lambda b, i: (b, SEQ // CTX_LEN, 0))
    return pl.pallas_call(
        _na_kernel,
        grid=(BATCH, n_rb),
        in_specs=[blk(cur), blk(prev), blk(cur), blk(nxt), blk(prev), blk(cur), blk(nxt),
                  ctx, ctx,
                  pl.BlockSpec((1,) + tbl.shape[1:], lambda b, i: (l, 0, 0, 0, 0))],
        out_specs=blk(cur),
        out_shape=jax.ShapeDtypeStruct((BATCH, SEQ, NA_W), _BF16),
        scratch_shapes=[pltpu.VMEM((3 * NA_QT, NA_W), _BF16),
                        pltpu.VMEM((3 * NA_QT, NA_W), _BF16)],
        compiler_params=pltpu.CompilerParams(
            dimension_semantics=("arbitrary", "arbitrary"), vmem_limit_bytes=VMEM_LIMIT),
        name="na_attention",
    )(qa, ka, ka, ka, va, va, va, ka, va, tbl)


def _stack_heads(q, m_lo, m_hi):
    parts = []
    for pos in range(WA_HEADS):
        qp = q[:, (pos // 2) * LANES:(pos // 2 + 1) * LANES]
        parts.append(qp * (m_lo if pos % 2 == 0 else m_hi))
    return jnp.concatenate(parts, axis=0)


def _unstack_heads(o, rows, lo):
    outs = []
    for g in range(WA_HEADS // 2):
        outs.append(jnp.where(lo, o[(2 * g) * rows:(2 * g + 1) * rows],
                              o[(2 * g + 1) * rows:(2 * g + 2) * rows]))
    return jnp.concatenate(outs, axis=1)


def _sink_column(sink_ref, rows):
    return jnp.concatenate(
        [jnp.full((rows, 1), sink_ref[pos], _F32) for pos in range(WA_HEADS)], axis=0)


def _wa_kernel(sink_ref, q_ref, kp_ref, kc_ref, kn_ref, vp_ref, vc_ref, vn_ref, kx_ref, vx_ref,
               o_ref, kbuf, vbuf):
    i = pl.program_id(1)
    kbuf[0:WA_BLOCK] = kp_ref[0]
    kbuf[WA_BLOCK:WA_BLOCK + WA_QT] = kc_ref[0]
    kbuf[WA_BLOCK + WA_QT:] = kn_ref[0]
    vbuf[0:WA_BLOCK] = vp_ref[0]
    vbuf[WA_BLOCK:WA_BLOCK + WA_QT] = vc_ref[0]
    vbuf[WA_BLOCK + WA_QT:] = vn_ref[0]
    lo, m_lo, m_hi = _lane_masks()
    n_loc = 3 * WA_BLOCK
    q_off = lax.broadcasted_iota(jnp.int32, (WA_BLOCK, n_loc), 0)
    k_off = lax.broadcasted_iota(jnp.int32, (WA_BLOCK, n_loc), 1) - WA_BLOCK
    band = jnp.abs(k_off - q_off) <= WA_WINDOW
    n_pairs = WA_HEADS // 2

    def geometry(u):
        t, g = divmod(u, n_pairs)
        return t, g, slice(t * WA_BLOCK, (t + 1) * WA_BLOCK), slice(g * LANES, (g + 1) * LANES)

    def sink_col(g):
        return jnp.concatenate(
            [jnp.full((WA_BLOCK, 1), sink_ref[2 * g + hh], _F32) for hh in range(2)], axis=0)

    def score_stage(u, _):
        t, _, rows, cols = geometry(u)
        qp = q_ref[0, rows, cols]
        q2 = jnp.concatenate([qp * m_lo, qp * m_hi], axis=0)
        s_loc = lax.dot_general(q2, kbuf[t * WA_BLOCK:t * WA_BLOCK + n_loc, :], _NT,
                                preferred_element_type=_F32)
        s_cx = lax.dot_general(q2, kx_ref[0], _NT, preferred_element_type=_F32)
        return s_loc, s_cx

    def max_stage(u, s):
        t, g, _, _ = geometry(u)
        s_loc, s_cx = s
        k_pos = (i * WA_SUB + t) * WA_BLOCK + k_off
        valid = band & (k_pos >= 0) & (k_pos < SEQ)
        s_loc = jnp.where(valid[None], s_loc.reshape(2, WA_BLOCK, n_loc), NEG_INF)
        sc = [s_loc.reshape(2 * WA_BLOCK, n_loc), s_cx]
        return sc, _row_max(sc, sink_col(g))

    def pv_stage(u, s):
        t, g, _, _ = geometry(u)
        sc, m = s
        return _exp_pv(sc, m, [vbuf[t * WA_BLOCK:t * WA_BLOCK + n_loc, :], vx_ref[0]],
                       sink_col(g))

    def out_stage(u, s):
        _, _, rows, cols = geometry(u)
        o2 = _normalise(s)
        o_ref[0, rows, cols] = jnp.where(lo, o2[:WA_BLOCK], o2[WA_BLOCK:]).astype(_BF16)

    _emit_staged(WA_SUB * n_pairs, [score_stage, max_stage, pv_stage, out_stage], STAGE_LAGS)


def _wa_attention(qb, kb, vb, sink):
    n_blk = SEQ // WA_BLOCK
    qspec = pl.BlockSpec((1, WA_QT, WA_W), lambda b, i: (b, i, 0))
    cur = pl.BlockSpec((1, WA_QT, WA_KV_W), lambda b, i: (b, i, 0))
    prev = pl.BlockSpec((1, WA_BLOCK, WA_KV_W),
                        lambda b, i: (b, jnp.maximum(i * WA_SUB - 1, 0), 0))
    nxt = pl.BlockSpec((1, WA_BLOCK, WA_KV_W),
                       lambda b, i: (b, jnp.minimum((i + 1) * WA_SUB, n_blk - 1), 0))
    ctx = pl.BlockSpec((1, CTX_LEN, WA_KV_W), lambda b, i: (b, SEQ // CTX_LEN, 0))
    return pl.pallas_call(
        _wa_kernel,
        grid=(BATCH, SEQ // WA_QT),
        in_specs=[pl.BlockSpec(memory_space=pltpu.SMEM),
                  qspec, prev, cur, nxt, prev, cur, nxt, ctx, ctx],
        out_specs=qspec,
        out_shape=jax.ShapeDtypeStruct((BATCH, SEQ, WA_W), _BF16),
        scratch_shapes=[pltpu.VMEM((WA_QT + 2 * WA_BLOCK, WA_KV_W), _BF16),
                        pltpu.VMEM((WA_QT + 2 * WA_BLOCK, WA_KV_W), _BF16)],
        compiler_params=pltpu.CompilerParams(
            dimension_semantics=("arbitrary", "arbitrary"), vmem_limit_bytes=VMEM_LIMIT),
        name="wa_attention",
    )(sink, qb, kb, kb, kb, vb, vb, vb, kb, vb)


def _ctx_kernel(sink_ref, qa_ref, ka_ref, va_ref, qb_ref, kb_ref, vb_ref, oa_ref, ob_ref):
    lo, m_lo, m_hi = _lane_masks()
    qa = qa_ref[0]
    outs = []
    for p in range(NA_HEADS // 2):
        cols = slice(p * LANES, (p + 1) * LANES)
        qp = qa[:, cols]
        q2 = jnp.concatenate([qp * m_lo, qp * m_hi], axis=0)
        s = lax.dot_general(q2, ka_ref[0, :, cols], _NT, preferred_element_type=_F32)
        o2 = _softmax_pv([s], [va_ref[0, :, cols]])
        outs.append(jnp.where(lo, o2[:CTX_LEN], o2[CTX_LEN:]))
    oa_ref[0] = jnp.concatenate(outs, axis=1).astype(_BF16)

    q8 = _stack_heads(qb_ref[0], m_lo, m_hi)
    s = lax.dot_general(q8, kb_ref[0], _NT, preferred_element_type=_F32)
    o8 = _softmax_pv([s], [vb_ref[0]], extra_logit=_sink_column(sink_ref, CTX_LEN))
    ob_ref[0] = _unstack_heads(o8, CTX_LEN, lo).astype(_BF16)


def _ctx_attention(qa, ka, va, qb, kb, vb, sink):
    ctx = lambda w: pl.BlockSpec((1, CTX_LEN, w), lambda b: (b, SEQ // CTX_LEN, 0))
    out = lambda w: pl.BlockSpec((1, CTX_LEN, w), lambda b: (b, 0, 0))
    return pl.pallas_call(
        _ctx_kernel,
        grid=(BATCH,),
        in_specs=[pl.BlockSpec(memory_space=pltpu.SMEM),
                  ctx(NA_W), ctx(NA_W), ctx(NA_W), ctx(WA_W), ctx(WA_KV_W), ctx(WA_KV_W)],
        out_specs=[out(NA_W), out(WA_W)],
        out_shape=[jax.ShapeDtypeStruct((BATCH, CTX_LEN, NA_W), _BF16),
                   jax.ShapeDtypeStruct((BATCH, CTX_LEN, WA_W), _BF16)],
        compiler_params=pltpu.CompilerParams(
            dimension_semantics=("arbitrary",), vmem_limit_bytes=VMEM_LIMIT),
        name="ctx_attention",
    )(sink, qa, ka, va, qb, kb, vb)


def _rope_tables():
    rot = HEAD_DIM // 2
    inv_freq = ROPE_BASE ** (-jnp.arange(0, rot, 2, dtype=_F32) / rot)
    t = jnp.arange(SEQ)
    row = (t // GRID_W).astype(_F32)
    col = (t % GRID_W).astype(_F32)
    ang = jnp.stack([row[:, None] * inv_freq, col[:, None] * inv_freq], axis=1)
    d = np.arange(LANES) % HEAD_DIM
    axis, half, freq = d // rot, (d % rot) // (rot // 2), d % (rot // 2)
    ang_l = ang[:, axis, freq]
    sign = jnp.asarray(np.where(half == 0, -1.0, 1.0), _F32)
    cos_t = jnp.concatenate([jnp.cos(ang_l), jnp.ones((CTX_LEN, LANES), _F32)], axis=0)
    sin_t = jnp.concatenate([jnp.sin(ang_l) * sign, jnp.zeros((CTX_LEN, LANES), _F32)], axis=0)
    return cos_t, sin_t


def _na_bias_tables(rpb):
    col = np.arange(GRID_W)
    c0 = np.clip(col - NA_KW // 2, 0, GRID_W - NA_KW)
    in_win = (col[None, :] >= c0[:, None]) & (col[None, :] < c0[:, None] + NA_KW)
    dc = np.clip(col[None, :] - col[:, None] + (NA_KW - 1), 0, 2 * NA_KW - 2)
    onehot = jnp.asarray(np.arange(2 * NA_KW - 1)[:, None] == dc.reshape(1, -1), _F32)
    full = jnp.dot(rpb.astype(_F32).reshape(-1, 2 * NA_KW - 1), onehot,
                   precision=lax.Precision.HIGHEST)
    full = full.reshape(DEPTH, NA_HEADS, 2 * NA_KH - 1, GRID_W, GRID_W)
    full = jnp.where(in_win, full * LOG2E, NEG_INF)
    return jnp.concatenate([full[:, :, :-1], full[:, :, 1:]], axis=-1)


def kernel(x, c, ctx, c_ctx, w_ada, b_ada, norm_g, w_ffn_up, w_ffn_down, w_in, w_out,
           qk_norm_g, na_rpb, wa_sink):
    group = WA_HEADS // WA_KV_HEADS
    w_up_b = w_ffn_up.astype(_BF16)
    w_dn_b = w_ffn_down.astype(_BF16)
    w_in_b = w_in.astype(_BF16)
    wq_b = (w_in_b[:, :, NA_W:Q_W].reshape(DEPTH, D_MODEL, WA_KV_HEADS, group, HEAD_DIM)
            .transpose(0, 1, 3, 2, 4).reshape(DEPTH, D_MODEL, WA_W))
    w_in_b = jnp.concatenate([w_in_b[:, :, :NA_W], wq_b, w_in_b[:, :, Q_W:]], axis=2)
    w_out_b = w_out.astype(_BF16)
    wo_b = (w_out_b[:, NA_W:].reshape(DEPTH, WA_KV_HEADS, group, HEAD_DIM, D_MODEL)
            .transpose(0, 2, 1, 3, 4).reshape(DEPTH, WA_W, D_MODEL))
    w_out_b = jnp.concatenate([w_out_b[:, :NA_W], wo_b], axis=1)
    sink_p = ((wa_sink.astype(_F32) * LOG2E).reshape(DEPTH, WA_KV_HEADS, group)
              .transpose(0, 2, 1).reshape(DEPTH, WA_HEADS))
    g4 = jnp.tile(qk_norm_g.astype(_F32), (1, 1, NA_W // HEAD_DIM))
    blk = np.arange(MXU_TILE) // HEAD_DIM
    bd = jnp.asarray((blk[:, None] == blk[None, :]) / HEAD_DIM, _BF16)
    cos_t, sin_t = _rope_tables()
    na_tbl = _na_bias_tables(na_rpb)

    cond = jnp.zeros((N_COND, D_MODEL), _F32).at[:BATCH].set(c).at[CTX_COND].set(c_ctx)
    mods = _ada_params(cond, w_ada, b_ada).reshape(DEPTH, N_COND, N_MOD, D_MODEL)

    x_all = (x, ctx)
    for l in range(DEPTH):
        last = l == DEPTH - 1
        mod_l = mods[l]
        x_all, qa, qb, ka, va, kb, vb = _ffn(
            x_all, mod_l, norm_g[l, 0:1], w_up_b, w_dn_b, l, 0, N_TILES,
            proj=(norm_g[l, 1:2], w_in_b, g4[l], bd, cos_t, sin_t))
        attn_a = _na_attention(qa, ka, va, na_tbl, l)
        attn_b = _wa_attention(qb, kb, vb, sink_p[l])
        if last:
            ctx_a = jnp.zeros((BATCH, CTX_LEN, NA_W), _BF16)
            ctx_b = jnp.zeros((BATCH, CTX_LEN, WA_W), _BF16)
        else:
            ctx_a, ctx_b = _ctx_attention(qa, ka, va, qb, kb, vb, sink_p[l])
        x_all = _ffn(x_all, mod_l, norm_g[l, 2:3], w_up_b, w_dn_b, l, 1,
                     CTX_TILE if last else N_TILES, mix=(attn_a, attn_b, ctx_a, ctx_b, w_out_b))
    return x_all
```
